```python
import jax, jax.numpy as jnp
from jax import lax
import numpy as np

D_MODEL = 1024
BATCH = 16
SEQ = 256
DEPTH = 4
DEC_BATCH = 4
DEC_SEQ = 2048
PAST_LEN = 256

GRID_W = 64
N_HEADS = 8
N_KV_HEADS = 2
HEAD_DIM = 64
GQA_GROUP = N_HEADS // N_KV_HEADS
ATTN_WIDTH = N_HEADS * HEAD_DIM
KV_WIDTH = N_KV_HEADS * HEAD_DIM
POOL_WIDTH = D_MODEL - ATTN_WIDTH
POOL_WINDOWS = (2, 4, 8, 16)
N_POOL_GROUPS = len(POOL_WINDOWS)
POOL_GROUP_W = POOL_WIDTH // N_POOL_GROUPS
IN_WIDTH = ATTN_WIDTH + 2 * KV_WIDTH + ATTN_WIDTH + 2 * POOL_WIDTH
WINDOW = 128
BLOCK = 128
SPAN = BLOCK + 2 * WINDOW
AXIS_DIM = HEAD_DIM // 2
ROPE_THETA = 10000.0
EPS = 1e-6
NEG_INF = -1e30

kernel_name = "hybrid_dit_swa_pool_step"


def _rmsnorm(x, w):
    xf = x.astype(jnp.float32)
    r = lax.rsqrt(jnp.mean(xf * xf, axis=-1, keepdims=True) + EPS)
    return (xf * r).astype(x.dtype) * w


def _rope_2d(x, rows):
    row = jnp.repeat(jnp.arange(rows), GRID_W).astype(jnp.float32)
    col = jnp.tile(jnp.arange(GRID_W), rows).astype(jnp.float32)
    inv = ROPE_THETA ** (-(jnp.arange(0, AXIS_DIM, 2, dtype=jnp.float32) / AXIS_DIM))

    def rot(xa, pos):
        ang = pos[:, None] * inv[None, :]
        cos = jnp.cos(ang)[None, :, None, :].astype(xa.dtype)
        sin = jnp.sin(ang)[None, :, None, :].astype(xa.dtype)
        x1, x2 = xa[..., : AXIS_DIM // 2], xa[..., AXIS_DIM // 2:]
        return jnp.concatenate([x1 * cos - x2 * sin, x2 * cos + x1 * sin], axis=-1)

    return jnp.concatenate([rot(x[..., :AXIS_DIM], row), rot(x[..., AXIS_DIM:], col)], axis=-1)


def _attend(qb, k, v, mask, sink):
    s = jnp.einsum("bqkgd,bnkd->bkgqn", qb, k).astype(jnp.float32) * (HEAD_DIM ** -0.5)
    if mask is not None:
        s = jnp.where(mask[None, None, None], s, NEG_INF)
    sk = sink.astype(jnp.float32)[None, :, :, None, None]
    m = jnp.maximum(jnp.max(s, axis=-1, keepdims=True), sk)
    e = jnp.exp(s - m)
    p = e / (jnp.sum(e, axis=-1, keepdims=True) + jnp.exp(sk - m))
    return jnp.einsum("bkgqn,bnkd->bqkgd", p.astype(v.dtype), v)


def _context_attention(q, k, v, sink):
    B, L = q.shape[0], q.shape[1]
    nb = L // BLOCK
    qs = jnp.moveaxis(q.reshape(B, nb, BLOCK, N_KV_HEADS, GQA_GROUP, HEAD_DIM), 1, 0)
    sk = sink.reshape(N_KV_HEADS, GQA_GROUP)
    out = lax.map(lambda qb: _attend(qb, k, v, None, sk), qs)
    return jnp.moveaxis(out, 0, 1).reshape(B, L, ATTN_WIDTH)


def _latent_attention(q, k, v, kc, vc, sink):
    B, L = q.shape[0], q.shape[1]
    Lc = kc.shape[1]
    nb = L // BLOCK
    pad = ((0, 0), (WINDOW, WINDOW), (0, 0), (0, 0))
    kp, vp = jnp.pad(k, pad), jnp.pad(v, pad)
    qs = jnp.moveaxis(q.reshape(B, nb, BLOCK, N_KV_HEADS, GQA_GROUP, HEAD_DIM), 1, 0)
    sk = sink.reshape(N_KV_HEADS, GQA_GROUP)
    qi = jnp.arange(BLOCK)[:, None]
    rj = jnp.arange(SPAN)[None, :]
    ctx_mask = jnp.ones((BLOCK, Lc), dtype=bool)

    def one(args):
        b, qb = args
        start = b * BLOCK
        kb = lax.dynamic_slice_in_dim(kp, start, SPAN, axis=1)
        vb = lax.dynamic_slice_in_dim(vp, start, SPAN, axis=1)
        qpos = start + qi
        kpos = start - WINDOW + rj
        band = (jnp.abs(qpos - kpos) <= WINDOW) & (kpos >= 0) & (kpos < L)
        mask = jnp.concatenate([band, ctx_mask], axis=1)
        return _attend(qb, jnp.concatenate([kb, kc], axis=1),
                       jnp.concatenate([vb, vc], axis=1), mask, sk)

    out = lax.map(one, (jnp.arange(nb), qs))
    return jnp.moveaxis(out, 0, 1).reshape(B, L, ATTN_WIDTH)


def _pool_mixer(u, w_pool, pool_scale):
    B, L = u.shape[0], u.shape[1]
    t = jnp.arange(L)
    ug = u.reshape(B, L, N_POOL_GROUPS, POOL_GROUP_W)
    outs = []
    for g, w in enumerate(POOL_WINDOWS):
        xg = ug[:, :, g].astype(jnp.float32)
        cs = jnp.concatenate([jnp.zeros((B, 1, POOL_GROUP_W), jnp.float32),
                              lax.cumsum(xg, axis=1)], axis=1)
        lo = jnp.clip(t - w // 2, 0, L)
        hi = jnp.clip(t + w // 2, 0, L)
        mean = (cs[:, hi] - cs[:, lo]) / (hi - lo).astype(jnp.float32)[None, :, None]
        outs.append(mean - xg)
    pooled = jnp.stack(outs, axis=2).astype(u.dtype)
    y = jnp.einsum("blgc,gcd->blgd", pooled, w_pool).reshape(B, L, POOL_WIDTH)
    return y * pool_scale


def _modulated_projection(x, mod, norm_w, w_in):
    shift, scale, gate = jnp.split(mod, 3, axis=-1)
    h = _rmsnorm(x, norm_w) * (1.0 + scale) + shift
    p = h @ w_in
    o1 = ATTN_WIDTH
    o2 = o1 + KV_WIDTH
    o3 = o2 + KV_WIDTH
    o4 = o3 + ATTN_WIDTH
    o5 = o4 + POOL_WIDTH
    q, k, v, ga, u, gp = jnp.split(p, [o1, o2, o3, o4, o5], axis=-1)
    B, L = x.shape[0], x.shape[1]
    q = q.reshape(B, L, N_HEADS, HEAD_DIM)
    k = k.reshape(B, L, N_KV_HEADS, HEAD_DIM)
    v = v.reshape(B, L, N_KV_HEADS, HEAD_DIM)
    return q, k, v, ga, u, gp, gate


def _merge_output(x, attn_out, ga, pool_out, gp, gate, attn_norm, pool_norm, w_out):
    a = _rmsnorm(attn_out, attn_norm) * jax.nn.silu(ga)
    pl = _rmsnorm(pool_out, pool_norm) * jax.nn.silu(gp)
    y = jnp.concatenate([a, pl], axis=-1) @ w_out
    return x + gate * y


def setup_inputs(seed: int = 0) -> dict:
    key = jax.random.key(seed)
    ks = jax.random.split(key, 20)
    f32 = jnp.float32
    D = D_MODEL
    cache_shape = (DEC_BATCH, DEPTH, PAST_LEN, N_KV_HEADS, HEAD_DIM)
    return {
        "x_prompt": jax.random.normal(ks[0], (BATCH, SEQ, D), f32),
        "x_sample": jax.random.normal(ks[1], (DEC_BATCH, DEC_SEQ, D), f32),
        "cache_k": jax.random.normal(ks[2], cache_shape, f32),
        "cache_v": jax.random.normal(ks[3], cache_shape, f32),
        "c": jax.random.normal(ks[4], (DEC_BATCH, D), f32),
        "c_ctx": jax.random.normal(ks[5], (D,), f32),
        "norm_w": 1.0 + 0.02 * jax.random.normal(ks[6], (DEPTH, D), f32),
        "w_ada": 0.5 * D ** -0.5 * jax.random.normal(ks[7], (DEPTH, D, 3 * D), f32),
        "b_ada": 0.02 * jax.random.normal(ks[8], (DEPTH, 3 * D), f32),
        "w_in": D ** -0.5 * jax.random.normal(ks[9], (DEPTH, D, IN_WIDTH), f32),
        "sink": 0.5 * jax.random.normal(ks[10], (DEPTH, N_HEADS), f32),
        "attn_norm": 1.0 + 0.02 * jax.random.normal(ks[11], (DEPTH, ATTN_WIDTH), f32),
        "pool_norm": 1.0 + 0.02 * jax.random.normal(ks[12], (DEPTH, POOL_WIDTH), f32),
        "w_pool": POOL_GROUP_W ** -0.5 * jax.random.normal(
            ks[13], (DEPTH, N_POOL_GROUPS, POOL_GROUP_W, POOL_GROUP_W), f32),
        "pool_scale": 1.0 + 0.1 * jax.random.normal(ks[14], (DEPTH, POOL_WIDTH), f32),
        "w_out": D ** -0.5 * jax.random.normal(ks[15], (DEPTH, D, D), f32),
        "final_norm": 1.0 + 0.02 * jax.random.normal(ks[16], (D,), f32),
    }


def reference(x_prompt, x_sample, cache_k, cache_v, c, c_ctx, norm_w, w_ada, b_ada,
              w_in, sink, attn_norm, pool_norm, w_pool, pool_scale, w_out, final_norm):
    silu_ctx = jax.nn.silu(c_ctx)
    silu_c = jax.nn.silu(c)
    xp, xs = x_prompt, x_sample
    rows = xs.shape[1] // GRID_W
    new_k, new_v = [], []
    for l in range(DEPTH):
        mod_ctx = (silu_ctx @ w_ada[l] + b_ada[l])[None, None, :]
        mod_lat = (silu_c @ w_ada[l] + b_ada[l])[:, None, :]

        q, k, v, ga, u, gp, gate = _modulated_projection(xp, mod_ctx, norm_w[l], w_in[l])
        attn = _context_attention(q, k, v, sink[l])
        pool = _pool_mixer(u, w_pool[l], pool_scale[l])
        xp = _merge_output(xp, attn, ga, pool, gp, gate, attn_norm[l], pool_norm[l], w_out[l])
        new_k.append(k)
        new_v.append(v)

        q, k, v, ga, u, gp, gate = _modulated_projection(xs, mod_lat, norm_w[l], w_in[l])
        q = _rope_2d(q, rows)
        k = _rope_2d(k, rows)
        attn = _latent_attention(q, k, v, cache_k[:, l], cache_v[:, l], sink[l])
        pool = _pool_mixer(u, w_pool[l], pool_scale[l])
        xs = _merge_output(xs, attn, ga, pool, gp, gate, attn_norm[l], pool_norm[l], w_out[l])

    y_prompt = _rmsnorm(xp, final_norm)
    y_sample = _rmsnorm(xs, final_norm)
    k_state = jnp.stack(new_k, axis=1)
    v_state = jnp.stack(new_v, axis=1)
    return (y_prompt, y_sample, k_state, v_state)
```

```python
import functools

import jax
import jax.numpy as jnp
from jax import lax
from jax.experimental import pallas as pl
from jax.experimental.pallas import tpu as pltpu

F32 = jnp.float32
BF16 = jnp.bfloat16

D_MODEL = 1024
N_HEADS = 8
HEAD_DIM = 64
ATTN_WIDTH = N_HEADS * HEAD_DIM
KV_WIDTH = 2 * HEAD_DIM
POOL_WIDTH = D_MODEL - ATTN_WIDTH
POOL_WINDOWS = (2, 4, 8, 16)
POOL_GROUP_W = POOL_WIDTH // len(POOL_WINDOWS)
IN_WIDTH = ATTN_WIDTH + 2 * KV_WIDTH + ATTN_WIDTH + 2 * POOL_WIDTH
GRID_W = 64
WINDOW = 128
QBLK = 128
AXIS_DIM = HEAD_DIM // 2
ROPE_THETA = 10000.0
EPS = 1e-6
NEG_INF = -1e30
LANES = 128
POOL_HALO = 8
VMEM_LIMIT = 56 * 1024 * 1024

O_Q = 0
O_K = ATTN_WIDTH
O_V = O_K + KV_WIDTH
O_GA = O_V + KV_WIDTH
O_U = O_GA + ATTN_WIDTH
O_GP = O_U + POOL_WIDTH


def _rms(x, w):
    r = lax.rsqrt(jnp.mean(x * x, axis=-1, keepdims=True) + EPS)
    return (x * r) * w


def _silu(x):
    return x * jax.nn.sigmoid(x)


def _ada_kernel(c_ref, w_ref, b_ref, o_ref):
    a = _silu(c_ref[...])
    w = w_ref[...]
    a_hi = a.astype(BF16)
    a_lo = (a - a_hi.astype(F32)).astype(BF16)
    w_hi = w.astype(BF16)
    w_lo = (w - w_hi.astype(F32)).astype(BF16)
    lhs = jnp.concatenate([a_hi, a_lo], axis=0)
    r = (jnp.dot(lhs, w_hi, preferred_element_type=F32)
         + jnp.dot(lhs, w_lo, preferred_element_type=F32))
    o_ref[...] = r[:8] + r[8:] + b_ref[...]


def _ada(cvec, w_ada, b_ada):
    depth = w_ada.shape[0]
    tn = 1024
    return pl.pallas_call(
        _ada_kernel,
        grid=(depth, 3 * D_MODEL // tn),
        in_specs=[
            pl.BlockSpec((8, D_MODEL), lambda l, j: (0, 0)),
            pl.BlockSpec((None, D_MODEL, tn), lambda l, j: (l, 0, j)),
            pl.BlockSpec((None, 1, tn), lambda l, j: (l, 0, j)),
        ],
        out_specs=pl.BlockSpec((None, 8, tn), lambda l, j: (l, 0, j)),
        out_shape=jax.ShapeDtypeStruct((depth, 8, 3 * D_MODEL), F32),
        compiler_params=pltpu.CompilerParams(
            dimension_semantics=("parallel", "parallel"), vmem_limit_bytes=VMEM_LIMIT),
    )(cvec, w_ada, b_ada.reshape(depth, 1, 3 * D_MODEL))


def _rope(z, cos, sin):
    lane = lax.broadcasted_iota(jnp.int32, z.shape, 1)
    first = (lane % (2 * 16)) < 16
    partner = jnp.where(first, pltpu.roll(z, LANES - 16, 1), pltpu.roll(z, 16, 1))
    return z * cos + partner * sin


def _proj_kernel(*refs, rope, emit_kv):
    x_ref, mod_ref, nw_ref, w_ref = refs[:4]
    refs = refs[4:]
    if rope:
        cos_ref, sin_ref = refs[:2]
        refs = refs[2:]
    q_ref, kv_ref, ga_ref, u_ref, gp_ref = refs[:5]
    refs = refs[5:]

    x = x_ref[...]
    shift = mod_ref[:, 0:D_MODEL]
    scale = mod_ref[:, D_MODEL:2 * D_MODEL]
    h = _rms(x, nw_ref[...]) * (1.0 + scale) + shift
    hb = h.astype(BF16)

    def project(lo, width):
        return jnp.dot(hb, w_ref[:, lo:lo + width], preferred_element_type=F32)

    q = project(O_Q, ATTN_WIDTH)
    k = project(O_K, KV_WIDTH)
    v = project(O_V, KV_WIDTH)
    if emit_kv:
        k_ref, v_ref = refs
        k_ref[...] = k
        v_ref[...] = v
    if rope:
        cos = cos_ref[...]
        sin = sin_ref[...]
        q = jnp.concatenate(
            [_rope(q[:, j * LANES:(j + 1) * LANES], cos, sin) for j in range(ATTN_WIDTH // LANES)], axis=1)
        k = _rope(k, cos, sin)
    q_ref[...] = (q * (HEAD_DIM ** -0.5)).astype(BF16)
    kv_ref[...] = jnp.concatenate(
        [k, pltpu.roll(k, HEAD_DIM, 1), v, pltpu.roll(v, HEAD_DIM, 1)], axis=1).astype(BF16)
    ga_ref[...] = project(O_GA, ATTN_WIDTH)
    u_ref[...] = project(O_U, POOL_WIDTH)
    gp_ref[...] = project(O_GP, POOL_WIDTH)


def _proj(x, mods, layer, row_of_batch, norm_w, w_in_b, rope_tabs, emit_kv, tm):
    nb, seq, _ = x.shape
    rope = rope_tabs is not None
    grid = (nb, seq // tm)
    tok = lambda width: pl.BlockSpec((None, tm, width), lambda b, t: (b, t, 0))
    in_specs = [
        tok(D_MODEL),
        pl.BlockSpec((None, None, 1, 3 * D_MODEL), lambda b, t: (layer, row_of_batch(b), 0, 0)),
        pl.BlockSpec((None, 1, D_MODEL), lambda b, t: (layer, 0, 0)),
        pl.BlockSpec((None, D_MODEL, IN_WIDTH), lambda b, t: (layer, 0, 0)),
    ]
    args = [x, mods, norm_w, w_in_b]
    if rope:
        in_specs += [pl.BlockSpec((tm, LANES), lambda b, t: (t, 0))] * 2
        args += list(rope_tabs)
    out_specs = [tok(ATTN_WIDTH), tok(4 * KV_WIDTH), tok(ATTN_WIDTH), tok(POOL_WIDTH), tok(POOL_WIDTH)]
    out_shape = [
        jax.ShapeDtypeStruct((nb, seq, ATTN_WIDTH), BF16),
        jax.ShapeDtypeStruct((nb, seq, 4 * KV_WIDTH), BF16),
        jax.ShapeDtypeStruct((nb, seq, ATTN_WIDTH), F32),
        jax.ShapeDtypeStruct((nb, seq, POOL_WIDTH), F32),
        jax.ShapeDtypeStruct((nb, seq, POOL_WIDTH), F32),
    ]
    if emit_kv:
        out_specs += [tok(KV_WIDTH), tok(KV_WIDTH)]
        out_shape += [jax.ShapeDtypeStruct((nb, seq, KV_WIDTH), F32)] * 2
    return pl.pallas_call(
        functools.partial(_proj_kernel, rope=rope, emit_kv=emit_kv),
        grid=grid, in_specs=in_specs, out_specs=out_specs, out_shape=out_shape,
        compiler_params=pltpu.CompilerParams(
            dimension_semantics=("parallel", "parallel"), vmem_limit_bytes=VMEM_LIMIT),
    )(*args)


def _head(qh, kk, vv, mask, sk):
    s = lax.dot_general(qh, kk, (((1,), (1,)), ((), ())), preferred_element_type=F32)
    if mask is not None:
        s = jnp.where(mask, s, NEG_INF)
    m = jnp.maximum(jnp.max(s, axis=-1, keepdims=True), sk)
    e = jnp.exp(s - m)
    den = jnp.sum(e, axis=-1, keepdims=True) + jnp.exp(sk - m)
    p = (e * (1.0 / den)).astype(BF16)
    return jnp.dot(p, vv, preferred_element_type=F32)


def _attend(q, k, ksw, v, vsw, mask, sink_of_head):
    lane = lax.broadcasted_iota(jnp.int32, (q.shape[0], LANES), 1)
    low = lane < HEAD_DIM
    outs = []
    for pair in range(N_HEADS // 2):
        qp = q[:, pair * LANES:(pair + 1) * LANES]
        zero = jnp.zeros_like(qp)
        if pair < N_HEADS // 4:
            k_even, k_odd, v_even, v_odd = k, ksw, v, vsw
        else:
            k_even, k_odd, v_even, v_odd = ksw, k, vsw, v
        o_even = _head(jnp.where(low, qp, zero), k_even, v_even, mask, sink_of_head(2 * pair))
        o_odd = _head(jnp.where(low, zero, qp), k_odd, v_odd, mask, sink_of_head(2 * pair + 1))
        outs.append(jnp.where(low, o_even, o_odd))
    return jnp.concatenate(outs, axis=1)


def _pool(u, u_prev, u_next, pos, seq_len, wp_ref, pool_scale):
    tm = u.shape[0]
    pad = jnp.zeros((POOL_HALO, POOL_GROUP_W), F32)
    rows = tm + 4 * POOL_HALO
    ys = []
    for g, w in enumerate(POOL_WINDOWS):
        sl = slice(g * POOL_GROUP_W, (g + 1) * POOL_GROUP_W)
        ug = u[:, sl]
        ext = jnp.concatenate([pad, u_prev[:, sl], ug, u_next[:, sl], pad], axis=0)
        s = ext + pltpu.roll(ext, 1, 0)
        step = 1
        while 2 * step < w:
            s = pltpu.roll(s, step, 0) + pltpu.roll(s, rows - step, 0)
            step *= 2
        ssum = s[2 * POOL_HALO:2 * POOL_HALO + tm]
        lo = jnp.clip(pos - w // 2, 0, seq_len)
        hi = jnp.clip(pos + w // 2, 0, seq_len)
        pooled = ssum / (hi - lo).astype(F32) - ug
        ys.append(jnp.dot(pooled.astype(BF16), wp_ref[g], preferred_element_type=F32))
    return jnp.concatenate(ys, axis=1) * pool_scale


def _merge(x, attn, ga, pool, gp, gate, an, pn, wo_ref, fn, final):
    a = _rms(attn, an) * _silu(ga)
    p = _rms(pool, pn) * _silu(gp)
    cat = jnp.concatenate([a, p], axis=1).astype(BF16)
    y = jnp.dot(cat, wo_ref[...], preferred_element_type=F32)
    out = x + gate * y
    if final:
        out = _rms(out, fn)
    return out


def _mix_ctx_kernel(sink_ref, q_ref, kv_ref, ga_ref, u_ref, gp_ref, x_ref, mod_ref, an_ref, pn_ref,
                    ps_ref, wp_ref, wo_ref, fn_ref, o_ref, attn_ref, *, layer, final):
    tm = q_ref.shape[0]
    kv = kv_ref[...]
    k, ksw, v, vsw = (kv[:, i * KV_WIDTH:(i + 1) * KV_WIDTH] for i in range(4))
    sink_of_head = lambda h: sink_ref[layer, h]
    for sub in range(tm // QBLK):
        rows = slice(sub * QBLK, (sub + 1) * QBLK)
        attn_ref[rows, :] = _attend(q_ref[rows, :], k, ksw, v, vsw, None, sink_of_head)
    pos = lax.broadcasted_iota(jnp.int32, (tm, 1), 0)
    halo = jnp.zeros((POOL_HALO, POOL_WIDTH), F32)
    pool = _pool(u_ref[...], halo, halo, pos, tm, wp_ref, ps_ref[...])
    o_ref[...] = _merge(x_ref[...], attn_ref[...], ga_ref[...], pool, gp_ref[...],
                        mod_ref[:, 2 * D_MODEL:], an_ref[...], pn_ref[...], wo_ref, fn_ref[...], final)


def _mix_lat_kernel(sink_ref, q_ref, kvc_ref, kvp_ref, kvn_ref, ck_ref, cv_ref, ga_ref, u_ref, up_ref,
                    un_ref, gp_ref, x_ref, mod_ref, an_ref, pn_ref, ps_ref, wp_ref, wo_ref, fn_ref,
                    o_ref, kx_ref, attn_ref, *, layer, final, seq_len):
    tm = q_ref.shape[0]
    t0 = pl.program_id(1) * tm
    kx_ref[0:QBLK, :] = kvp_ref[...]
    kx_ref[QBLK:QBLK + tm, :] = kvc_ref[...]
    kx_ref[QBLK + tm:, :] = kvn_ref[...]
    ck = ck_ref[...]
    cv = cv_ref[...]
    ckb, ckswb = ck.astype(BF16), pltpu.roll(ck, HEAD_DIM, 1).astype(BF16)
    cvb, cvswb = cv.astype(BF16), pltpu.roll(cv, HEAD_DIM, 1).astype(BF16)
    n_ctx = ck.shape[0]
    sink_of_head = lambda h: sink_ref[layer, h]
    span = QBLK + 2 * WINDOW
    qi = lax.broadcasted_iota(jnp.int32, (QBLK, span + n_ctx), 0)
    kj = lax.broadcasted_iota(jnp.int32, (QBLK, span + n_ctx), 1)
    for sub in range(tm // QBLK):
        rows = slice(sub * QBLK, (sub + 1) * QBLK)
        band = kx_ref[sub * QBLK:sub * QBLK + span, :]
        k, ksw, v, vsw = (band[:, i * KV_WIDTH:(i + 1) * KV_WIDTH] for i in range(4))
        kpos = t0 + (sub - 1) * QBLK + kj
        d = kj - qi
        mask = (kj >= span) | ((d >= 0) & (d <= 2 * WINDOW) & (kpos >= 0) & (kpos < seq_len))
        attn_ref[rows, :] = _attend(
            q_ref[rows, :],
            jnp.concatenate([k, ckb], axis=0), jnp.concatenate([ksw, ckswb], axis=0),
            jnp.concatenate([v, cvb], axis=0), jnp.concatenate([vsw, cvswb], axis=0),
            mask, sink_of_head)
    pos = t0 + lax.broadcasted_iota(jnp.int32, (tm, 1), 0)
    u_prev = jnp.where(t0 > 0, up_ref[...], 0.0)
    u_next = jnp.where(t0 + tm < seq_len, un_ref[...], 0.0)
    pool = _pool(u_ref[...], u_prev, u_next, pos, seq_len, wp_ref, ps_ref[...])
    o_ref[...] = _merge(x_ref[...], attn_ref[...], ga_ref[...], pool, gp_ref[...],
                        mod_ref[:, 2 * D_MODEL:], an_ref[...], pn_ref[...], wo_ref, fn_ref[...], final)


def _param_specs(layer):
    return [
        pl.BlockSpec((None, 1, ATTN_WIDTH), lambda b, t: (layer, 0, 0)),
        pl.BlockSpec((None, 1, POOL_WIDTH), lambda b, t: (layer, 0, 0)),
        pl.BlockSpec((None, 1, POOL_WIDTH), lambda b, t: (layer, 0, 0)),
        pl.BlockSpec((None, len(POOL_WINDOWS), POOL_GROUP_W, POOL_GROUP_W), lambda b, t: (layer, 0, 0, 0)),
        pl.BlockSpec((None, D_MODEL, D_MODEL), lambda b, t: (layer, 0, 0)),
        pl.BlockSpec((1, D_MODEL), lambda b, t: (0, 0)),
    ]


def _mix_ctx(x, q, kv, ga, u, gp, mods, layer, params, sink, final):
    nb, tm, _ = x.shape
    tok = lambda width: pl.BlockSpec((None, tm, width), lambda b, t: (b, 0, 0))
    in_specs = [
        pl.BlockSpec(memory_space=pltpu.SMEM),
        tok(ATTN_WIDTH), tok(4 * KV_WIDTH), tok(ATTN_WIDTH), tok(POOL_WIDTH), tok(POOL_WIDTH), tok(D_MODEL),
        pl.BlockSpec((None, None, 1, 3 * D_MODEL), lambda b, t: (layer, 0, 0, 0)),
    ] + _param_specs(layer)
    return pl.pallas_call(
        functools.partial(_mix_ctx_kernel, layer=layer, final=final),
        grid=(nb, 1), in_specs=in_specs, out_specs=tok(D_MODEL),
        out_shape=jax.ShapeDtypeStruct(x.shape, F32),
        scratch_shapes=[pltpu.VMEM((tm, ATTN_WIDTH), F32)],
        compiler_params=pltpu.CompilerParams(
            dimension_semantics=("parallel", "arbitrary"), vmem_limit_bytes=VMEM_LIMIT),
    )(sink, q, kv, ga, u, gp, x, mods, *params)


def _mix_lat(x, q, kv, ga, u, gp, cache_k, cache_v, mods, layer, params, sink, final, tm):
    nb, seq, _ = x.shape
    nq = seq // QBLK
    nh = seq // POOL_HALO
    tok = lambda width: pl.BlockSpec((None, tm, width), lambda b, t: (b, t, 0))
    kv_w = 4 * KV_WIDTH
    n_ctx = cache_k.shape[2]
    in_specs = [
        pl.BlockSpec(memory_space=pltpu.SMEM),
        tok(ATTN_WIDTH),
        tok(kv_w),
        pl.BlockSpec((None, QBLK, kv_w), lambda b, t: (b, jnp.maximum(t * (tm // QBLK) - 1, 0), 0)),
        pl.BlockSpec((None, QBLK, kv_w), lambda b, t: (b, jnp.minimum((t + 1) * (tm // QBLK), nq - 1), 0)),
        pl.BlockSpec((None, None, n_ctx, KV_WIDTH), lambda b, t: (b, layer, 0, 0)),
        pl.BlockSpec((None, None, n_ctx, KV_WIDTH), lambda b, t: (b, layer, 0, 0)),
        tok(ATTN_WIDTH),
        tok(POOL_WIDTH),
        pl.BlockSpec((None, POOL_HALO, POOL_WIDTH),
                     lambda b, t: (b, jnp.maximum(t * (tm // POOL_HALO) - 1, 0), 0)),
        pl.BlockSpec((None, POOL_HALO, POOL_WIDTH),
                     lambda b, t: (b, jnp.minimum((t + 1) * (tm // POOL_HALO), nh - 1), 0)),
        tok(POOL_WIDTH),
        tok(D_MODEL),
        pl.BlockSpec((None, None, 1, 3 * D_MODEL), lambda b, t: (layer, b + 1, 0, 0)),
    ] + _param_specs(layer)
    return pl.pallas_call(
        functools.partial(_mix_lat_kernel, layer=layer, final=final, seq_len=seq),
        grid=(nb, seq // tm), in_specs=in_specs, out_specs=tok(D_MODEL),
        out_shape=jax.ShapeDtypeStruct(x.shape, F32),
        scratch_shapes=[pltpu.VMEM((tm + 2 * QBLK, kv_w), BF16), pltpu.VMEM((tm, ATTN_WIDTH), F32)],
        compiler_params=pltpu.CompilerParams(
            dimension_semantics=("parallel", "arbitrary"), vmem_limit_bytes=VMEM_LIMIT),
    )(sink, q, kv, kv, kv, cache_k, cache_v, ga, u, u, u, gp, x, mods, *params)


def _rope_tables(seq):
    t = jnp.arange(seq)
    row = (t // GRID_W).astype(F32)
    col = (t % GRID_W).astype(F32)
    inv = ROPE_THETA ** (-(jnp.arange(0, AXIS_DIM, 2, dtype=F32) / AXIS_DIM))
    ang_r = row[:, None] * inv[None, :]
    ang_c = col[:, None] * inv[None, :]
    cos_h = jnp.concatenate([jnp.cos(ang_r)] * 2 + [jnp.cos(ang_c)] * 2, axis=1)
    sin_h = jnp.concatenate([-jnp.sin(ang_r), jnp.sin(ang_r), -jnp.sin(ang_c), jnp.sin(ang_c)], axis=1)
    return jnp.tile(cos_h, (1, LANES // HEAD_DIM)), jnp.tile(sin_h, (1, LANES // HEAD_DIM))


def kernel(x_prompt, x_sample, cache_k, cache_v, c, c_ctx, norm_w, w_ada, b_ada, w_in, sink, attn_norm,
           pool_norm, w_pool, pool_scale, w_out, final_norm):
    depth = w_in.shape[0]
    n_ctx_b, ctx_len, _ = x_prompt.shape
    n_lat_b, lat_len, _ = x_sample.shape
    assert n_lat_b + 1 <= 8

    cvec = jnp.concatenate([c_ctx[None, :], c, jnp.zeros((8 - 1 - n_lat_b, D_MODEL), F32)], axis=0)
    mods = _ada(cvec, w_ada, b_ada).reshape(depth, 8, 1, 3 * D_MODEL)

    w_in_b = w_in.astype(BF16)
    w_out_b = w_out.astype(BF16)
    w_pool_b = w_pool.astype(BF16)
    norm_w3 = norm_w.reshape(depth, 1, D_MODEL)
    params = (attn_norm.reshape(depth, 1, ATTN_WIDTH), pool_norm.reshape(depth, 1, POOL_WIDTH),
              pool_scale.reshape(depth, 1, POOL_WIDTH), w_pool_b, w_out_b, final_norm.reshape(1, D_MODEL))
    ck = cache_k.reshape(n_lat_b, depth, cache_k.shape[2], KV_WIDTH)
    cv = cache_v.reshape(n_lat_b, depth, cache_v.shape[2], KV_WIDTH)
    rope_tabs = _rope_tables(lat_len)

    xp, xs = x_prompt, x_sample
    new_k, new_v = [], []
    tm = 512
    for l in range(depth):
        final = l == depth - 1
        q, kv, ga, u, gp, k32, v32 = _proj(
            xp.reshape(1, n_ctx_b * ctx_len, D_MODEL), mods, l, lambda b: 0, norm_w3, w_in_b, None, True, tm)
        seq3 = lambda a: a.reshape(n_ctx_b, ctx_len, a.shape[-1])
        xp = _mix_ctx(xp, seq3(q), seq3(kv), seq3(ga), seq3(u), seq3(gp), mods, l, params, sink, final)
        new_k.append(seq3(k32))
        new_v.append(seq3(v32))

        q, kv, ga, u, gp = _proj(xs, mods, l, lambda b: b + 1, norm_w3, w_in_b, rope_tabs, False, tm)
        xs = _mix_lat(xs, q, kv, ga, u, gp, ck, cv, mods, l, params, sink, final, tm)

    kv_shape = (n_ctx_b, depth, ctx_len, KV_WIDTH // HEAD_DIM, HEAD_DIM)
    k_state = jnp.stack(new_k, axis=1).reshape(kv_shape)
    v_state = jnp.stack(new_v, axis=1).reshape(kv_shape)
    return (xp, xs, k_state, v_state)
```

```python
import functools

import jax
import jax.numpy as jnp
from jax import lax
from jax.experimental import pallas as pl
from jax.experimental.pallas import tpu as pltpu

F32 = jnp.float32
BF16 = jnp.bfloat16

D_MODEL = 1024
N_HEADS = 8
HEAD_DIM = 64
ATTN_WIDTH = N_HEADS * HEAD_DIM
KV_WIDTH = 2 * HEAD_DIM
POOL_WIDTH = D_MODEL - ATTN_WIDTH
POOL_WINDOWS = (2, 4, 8, 16)
POOL_GROUP_W = POOL_WIDTH // len(POOL_WINDOWS)
IN_WIDTH = ATTN_WIDTH + 2 * KV_WIDTH + ATTN_WIDTH + 2 * POOL_WIDTH
GRID_W = 64
WINDOW = 128
QBLK = 128
AXIS_DIM = HEAD_DIM // 2
ROPE_THETA = 10000.0
EPS = 1e-6
NEG_INF = -1e30
LOG2E = 1.4426950408889634
PIPE_DEPTH = 2
assert WINDOW == QBLK
LANES = 128
POOL_HALO = 8
VMEM_LIMIT = 56 * 1024 * 1024

O_Q = 0
O_K = ATTN_WIDTH
O_V = O_K + KV_WIDTH
O_GA = O_V + KV_WIDTH
O_U = O_GA + ATTN_WIDTH
O_GP = O_U + POOL_WIDTH


def _rms(x, w):
    r = lax.rsqrt(jnp.mean(x * x, axis=-1, keepdims=True) + EPS)
    return (x * r) * w


def _silu(x):
    return x * jax.nn.sigmoid(x)


def _ada_kernel(c_ref, w_ref, b_ref, o_ref):
    a = _silu(c_ref[...])
    w = w_ref[...]
    a_hi = a.astype(BF16)
    a_lo = (a - a_hi.astype(F32)).astype(BF16)
    w_hi = w.astype(BF16)
    w_lo = (w - w_hi.astype(F32)).astype(BF16)
    lhs = jnp.concatenate([a_hi, a_lo], axis=0)
    r = (jnp.dot(lhs, w_hi, preferred_element_type=F32)
         + jnp.dot(lhs, w_lo, preferred_element_type=F32))
    o_ref[...] = r[:8] + r[8:] + b_ref[...]


def _ada(cvec, w_ada, b_ada):
    depth = w_ada.shape[0]
    tn = 1024
    return pl.pallas_call(
        _ada_kernel,
        grid=(depth, 3 * D_MODEL // tn),
        in_specs=[
            pl.BlockSpec((8, D_MODEL), lambda l, j: (0, 0)),
            pl.BlockSpec((None, D_MODEL, tn), lambda l, j: (l, 0, j)),
            pl.BlockSpec((None, 1, tn), lambda l, j: (l, 0, j)),
        ],
        out_specs=pl.BlockSpec((None, 8, tn), lambda l, j: (l, 0, j)),
        out_shape=jax.ShapeDtypeStruct((depth, 8, 3 * D_MODEL), F32),
        compiler_params=pltpu.CompilerParams(
            dimension_semantics=("parallel", "parallel"), vmem_limit_bytes=VMEM_LIMIT),
    )(cvec, w_ada, b_ada.reshape(depth, 1, 3 * D_MODEL))


def _rope(z, cos, sin):
    lane = lax.broadcasted_iota(jnp.int32, z.shape, 1)
    first = (lane % (2 * 16)) < 16
    partner = jnp.where(first, pltpu.roll(z, LANES - 16, 1), pltpu.roll(z, 16, 1))
    return z * cos + partner * sin


def _proj_kernel(*refs, rope, emit_kv):
    x_ref, mod_ref, nw_ref, w_ref = refs[:4]
    refs = refs[4:]
    if rope:
        cos_ref, sin_ref = refs[:2]
        refs = refs[2:]
    q_ref, kv_ref, ga_ref, u_ref, gp_ref = refs[:5]
    refs = refs[5:]

    x = x_ref[...]
    shift = mod_ref[:, 0:D_MODEL]
    scale = mod_ref[:, D_MODEL:2 * D_MODEL]
    h = _rms(x, nw_ref[...]) * (1.0 + scale) + shift
    hb = h.astype(BF16)

    def project(lo, width):
        return jnp.dot(hb, w_ref[:, lo:lo + width], preferred_element_type=F32)

    q = project(O_Q, ATTN_WIDTH)
    k = project(O_K, KV_WIDTH)
    v = project(O_V, KV_WIDTH)
    if emit_kv:
        k_ref, v_ref = refs
        k_ref[...] = k
        v_ref[...] = v
    if rope:
        cos = cos_ref[...]
        sin = sin_ref[...]
        q = jnp.concatenate(
            [_rope(q[:, j * LANES:(j + 1) * LANES], cos, sin) for j in range(ATTN_WIDTH // LANES)], axis=1)
        k = _rope(k, cos, sin)
    q_ref[...] = (q * (HEAD_DIM ** -0.5 * LOG2E)).astype(BF16)
    kv_ref[...] = jnp.concatenate(
        [k, pltpu.roll(k, HEAD_DIM, 1), v, pltpu.roll(v, HEAD_DIM, 1)], axis=1).astype(BF16)
    ga_ref[...] = project(O_GA, ATTN_WIDTH)
    u_ref[...] = project(O_U, POOL_WIDTH)
    gp_ref[...] = project(O_GP, POOL_WIDTH)


def _proj(x, mods, layer, row_of_batch, norm_w, w_in_b, rope_tabs, emit_kv, tm):
    nb, seq, _ = x.shape
    rope = rope_tabs is not None
    grid = (nb, seq // tm)
    tok = lambda width: pl.BlockSpec((None, tm, width), lambda b, t: (b, t, 0))
    in_specs = [
        tok(D_MODEL),
        pl.BlockSpec((None, None, 1, 3 * D_MODEL), lambda b, t: (layer, row_of_batch(b), 0, 0)),
        pl.BlockSpec((None, 1, D_MODEL), lambda b, t: (layer, 0, 0)),
        pl.BlockSpec((None, D_MODEL, IN_WIDTH), lambda b, t: (layer, 0, 0)),
    ]
    args = [x, mods, norm_w, w_in_b]
    if rope:
        in_specs += [pl.BlockSpec((tm, LANES), lambda b, t: (t, 0))] * 2
        args += list(rope_tabs)
    out_specs = [tok(ATTN_WIDTH), tok(4 * KV_WIDTH), tok(ATTN_WIDTH), tok(POOL_WIDTH), tok(POOL_WIDTH)]
    out_shape = [
        jax.ShapeDtypeStruct((nb, seq, ATTN_WIDTH), BF16),
        jax.ShapeDtypeStruct((nb, seq, 4 * KV_WIDTH), BF16),
        jax.ShapeDtypeStruct((nb, seq, ATTN_WIDTH), F32),
        jax.ShapeDtypeStruct((nb, seq, POOL_WIDTH), F32),
        jax.ShapeDtypeStruct((nb, seq, POOL_WIDTH), F32),
    ]
    if emit_kv:
        out_specs += [tok(KV_WIDTH), tok(KV_WIDTH)]
        out_shape += [jax.ShapeDtypeStruct((nb, seq, KV_WIDTH), F32)] * 2
    return pl.pallas_call(
        functools.partial(_proj_kernel, rope=rope, emit_kv=emit_kv),
        grid=grid, in_specs=in_specs, out_specs=out_specs, out_shape=out_shape,
        compiler_params=pltpu.CompilerParams(
            dimension_semantics=("parallel", "parallel"), vmem_limit_bytes=VMEM_LIMIT),
    )(*args)


def _scores(qh, kk):
    return lax.dot_general(qh, kk, (((1,), (1,)), ((), ())), preferred_element_type=F32)


def _softmax_pv(s, biases, vv, sk2):
    if biases:
        blocks = [s[:, j * LANES:(j + 1) * LANES] for j in range(s.shape[1] // LANES)]
        s = jnp.concatenate([b + biases[j] if j in biases else b for j, b in enumerate(blocks)], axis=1)
    m = jnp.maximum(jnp.max(s, axis=-1, keepdims=True), sk2)
    e = jnp.exp2(s - m)
    den = jnp.sum(e, axis=-1, keepdims=True) + jnp.exp2(sk2 - m)
    o = jnp.dot(e.astype(BF16), vv, preferred_element_type=F32)
    return o * (1.0 / den)


def _attend(blocks, sink2_of_head, store):
    tq = blocks[0][0].shape[0]
    lane = lax.broadcasted_iota(jnp.int32, (tq, LANES), 1)
    low = lane < HEAD_DIM
    tasks = []
    for bi, (q, k, ksw, v, vsw, biases) in enumerate(blocks):
        for h in range(N_HEADS):
            odd = h % 2 == 1
            swapped = odd != (h >= N_HEADS // 2)
            tasks.append((bi, h, q, ksw if swapped else k, vsw if swapped else v, biases))

    def score(task):
        _, h, q, kk, _, _ = task
        qp = q[:, (h // 2) * LANES:(h // 2 + 1) * LANES]
        zero = jnp.zeros_like(qp)
        return _scores(jnp.where(low, zero, qp) if h % 2 else jnp.where(low, qp, zero), kk)

    pending = {i: score(tasks[i]) for i in range(min(PIPE_DEPTH, len(tasks)))}
    o_even = None
    for i, (bi, h, _, _, vv, biases) in enumerate(tasks):
        if i + PIPE_DEPTH < len(tasks):
            pending[i + PIPE_DEPTH] = score(tasks[i + PIPE_DEPTH])
        o = _softmax_pv(pending.pop(i), biases, vv, sink2_of_head(h))
        if h % 2:
            store(bi, h // 2, jnp.where(low, o_even, o))
        else:
            o_even = o


def _pool(u, u_prev, u_next, pos, seq_len, wp_ref, pool_scale):
    tm = u.shape[0]
    pad = jnp.zeros((POOL_HALO, POOL_GROUP_W), F32)
    rows = tm + 4 * POOL_HALO
    ys = []
    for g, w in enumerate(POOL_WINDOWS):
        sl = slice(g * POOL_GROUP_W, (g + 1) * POOL_GROUP_W)
        ug = u[:, sl]
        ext = jnp.concatenate([pad, u_prev[:, sl], ug, u_next[:, sl], pad], axis=0)
        s = ext + pltpu.roll(ext, 1, 0)
        step = 1
        while 2 * step < w:
            s = pltpu.roll(s, step, 0) + pltpu.roll(s, rows - step, 0)
            step *= 2
        ssum = s[2 * POOL_HALO:2 * POOL_HALO + tm]
        lo = jnp.clip(pos - w // 2, 0, seq_len)
        hi = jnp.clip(pos + w // 2, 0, seq_len)
        pooled = ssum / (hi - lo).astype(F32) - ug
        ys.append(jnp.dot(pooled.astype(BF16), wp_ref[g], preferred_element_type=F32))
    return jnp.concatenate(ys, axis=1) * pool_scale


def _merge(x, attn, ga, pool, gp, gate, an, pn, wo_ref, fn, final):
    a = _rms(attn, an) * _silu(ga)
    p = _rms(pool, pn) * _silu(gp)
    cat = jnp.concatenate([a, p], axis=1).astype(BF16)
    y = jnp.dot(cat, wo_ref[...], preferred_element_type=F32)
    out = x + gate * y
    if final:
        out = _rms(out, fn)
    return out


def _mix_ctx_kernel(sink_ref, q_ref, kv_ref, ga_ref, u_ref, gp_ref, x_ref, mod_ref, an_ref, pn_ref,
                    ps_ref, wp_ref, wo_ref, fn_ref, o_ref, attn_ref, *, layer, final):
    tm = q_ref.shape[0]
    kv = kv_ref[...]
    k, ksw, v, vsw = (kv[:, i * KV_WIDTH:(i + 1) * KV_WIDTH] for i in range(4))
    blocks = [(q_ref[sub * QBLK:(sub + 1) * QBLK, :], k, ksw, v, vsw, None) for sub in range(tm // QBLK)]

    def store(sub, pair, tile):
        attn_ref[sub * QBLK:(sub + 1) * QBLK, pair * LANES:(pair + 1) * LANES] = tile

    _attend(blocks, lambda h: sink_ref[layer, h] * LOG2E, store)
    pos = lax.broadcasted_iota(jnp.int32, (tm, 1), 0)
    halo = jnp.zeros((POOL_HALO, POOL_WIDTH), F32)
    pool = _pool(u_ref[...], halo, halo, pos, tm, wp_ref, ps_ref[...])
    o_ref[...] = _merge(x_ref[...], attn_ref[...], ga_ref[...], pool, gp_ref[...],
                        mod_ref[:, 2 * D_MODEL:], an_ref[...], pn_ref[...], wo_ref, fn_ref[...], final)


def _mix_lat_kernel(sink_ref, q_ref, kvc_ref, kvp_ref, kvn_ref, ck_ref, cv_ref, ga_ref, u_ref, up_ref,
                    un_ref, gp_ref, x_ref, mod_ref, an_ref, pn_ref, ps_ref, wp_ref, wo_ref, fn_ref,
                    o_ref, kx_ref, attn_ref, *, layer, final, seq_len):
    tm = q_ref.shape[0]
    t0 = pl.program_id(1) * tm
    kx_ref[0:QBLK, :] = kvp_ref[...]
    kx_ref[QBLK:QBLK + tm, :] = kvc_ref[...]
    kx_ref[QBLK + tm:, :] = kvn_ref[...]
    ck = ck_ref[...]
    cv = cv_ref[...]
    ckb, ckswb = ck.astype(BF16), pltpu.roll(ck, HEAD_DIM, 1).astype(BF16)
    cvb, cvswb = cv.astype(BF16), pltpu.roll(cv, HEAD_DIM, 1).astype(BF16)
    span = QBLK + 2 * WINDOW
    qi = lax.broadcasted_iota(jnp.int32, (QBLK, LANES), 0)
    kj = lax.broadcasted_iota(jnp.int32, (QBLK, LANES), 1)
    blocks = []
    for sub in range(tm // QBLK):
        band = kx_ref[sub * QBLK:sub * QBLK + span, :]
        k, ksw, v, vsw = (band[:, i * KV_WIDTH:(i + 1) * KV_WIDTH] for i in range(4))
        start = t0 + sub * QBLK
        biases = {
            0: jnp.where((kj >= qi) & (start - WINDOW >= 0), 0.0, NEG_INF),
            2: jnp.where((kj <= qi) & (start + QBLK + WINDOW <= seq_len), 0.0, NEG_INF),
        }
        blocks.append((q_ref[sub * QBLK:(sub + 1) * QBLK, :],
                       jnp.concatenate([k, ckb], axis=0), jnp.concatenate([ksw, ckswb], axis=0),
                       jnp.concatenate([v, cvb], axis=0), jnp.concatenate([vsw, cvswb], axis=0), biases))

    def store(sub, pair, tile):
        attn_ref[sub * QBLK:(sub + 1) * QBLK, pair * LANES:(pair + 1) * LANES] = tile

    _attend(blocks, lambda h: sink_ref[layer, h] * LOG2E, store)
    pos = t0 + lax.broadcasted_iota(jnp.int32, (tm, 1), 0)
    u_prev = jnp.where(t0 > 0, up_ref[...], 0.0)
    u_next = jnp.where(t0 + tm < seq_len, un_ref[...], 0.0)
    pool = _pool(u_ref[...], u_prev, u_next, pos, seq_len, wp_ref, ps_ref[...])
    o_ref[...] = _merge(x_ref[...], attn_ref[...], ga_ref[...], pool, gp_ref[...],
                        mod_ref[:, 2 * D_MODEL:], an_ref[...], pn_ref[...], wo_ref, fn_ref[...], final)


def _param_specs(layer):
    return [
        pl.BlockSpec((None, 1, ATTN_WIDTH), lambda b, t: (layer, 0, 0)),
        pl.BlockSpec((None, 1, POOL_WIDTH), lambda b, t: (layer, 0, 0)),
        pl.BlockSpec((None, 1, POOL_WIDTH), lambda b, t: (layer, 0, 0)),
        pl.BlockSpec((None, len(POOL_WINDOWS), POOL_GROUP_W, POOL_GROUP_W), lambda b, t: (layer, 0, 0, 0)),
        pl.BlockSpec((None, D_MODEL, D_MODEL), lambda b, t: (layer, 0, 0)),
        pl.BlockSpec((1, D_MODEL), lambda b, t: (0, 0)),
    ]


def _mix_ctx(x, q, kv, ga, u, gp, mods, layer, params, sink, final):
    nb, tm, _ = x.shape
    tok = lambda width: pl.BlockSpec((None, tm, width), lambda b, t: (b, 0, 0))
    in_specs = [
        pl.BlockSpec(memory_space=pltpu.SMEM),
        tok(ATTN_WIDTH), tok(4 * KV_WIDTH), tok(ATTN_WIDTH), tok(POOL_WIDTH), tok(POOL_WIDTH), tok(D_MODEL),
        pl.BlockSpec((None, None, 1, 3 * D_MODEL), lambda b, t: (layer, 0, 0, 0)),
    ] + _param_specs(layer)
    return pl.pallas_call(
        functools.partial(_mix_ctx_kernel, layer=layer, final=final),
        grid=(nb, 1), in_specs=in_specs, out_specs=tok(D_MODEL),
        out_shape=jax.ShapeDtypeStruct(x.shape, F32),
        scratch_shapes=[pltpu.VMEM((tm, ATTN_WIDTH), F32)],
        compiler_params=pltpu.CompilerParams(
            dimension_semantics=("parallel", "arbitrary"), vmem_limit_bytes=VMEM_LIMIT),
    )(sink, q, kv, ga, u, gp, x, mods, *params)


def _mix_lat(x, q, kv, ga, u, gp, cache_k, cache_v, mods, layer, params, sink, final, tm):
    nb, seq, _ = x.shape
    nq = seq // QBLK
    nh = seq // POOL_HALO
    tok = lambda width: pl.BlockSpec((None, tm, width), lambda b, t: (b, t, 0))
    kv_w = 4 * KV_WIDTH
    n_ctx = cache_k.shape[2]
    in_specs = [
        pl.BlockSpec(memory_space=pltpu.SMEM),
        tok(ATTN_WIDTH),
        tok(kv_w),
        pl.BlockSpec((None, QBLK, kv_w), lambda b, t: (b, jnp.maximum(t * (tm // QBLK) - 1, 0), 0)),
        pl.BlockSpec((None, QBLK, kv_w), lambda b, t: (b, jnp.minimum((t + 1) * (tm // QBLK), nq - 1), 0)),
        pl.BlockSpec((None, None, n_ctx, KV_WIDTH), lambda b, t: (b, layer, 0, 0)),
        pl.BlockSpec((None, None, n_ctx, KV_WIDTH), lambda b, t: (b, layer, 0, 0)),
        tok(ATTN_WIDTH),
        tok(POOL_WIDTH),
        pl.BlockSpec((None, POOL_HALO, POOL_WIDTH),
                     lambda b, t: (b, jnp.maximum(t * (tm // POOL_HALO) - 1, 0), 0)),
        pl.BlockSpec((None, POOL_HALO, POOL_WIDTH),
                     lambda b, t: (b, jnp.minimum((t + 1) * (tm // POOL_HALO), nh - 1), 0)),
        tok(POOL_WIDTH),
        tok(D_MODEL),
        pl.BlockSpec((None, None, 1, 3 * D_MODEL), lambda b, t: (layer, b + 1, 0, 0)),
    ] + _param_specs(layer)
    return pl.pallas_call(
        functools.partial(_mix_lat_kernel, layer=layer, final=final, seq_len=seq),
        grid=(nb, seq // tm), in_specs=in_specs, out_specs=tok(D_MODEL),
        out_shape=jax.ShapeDtypeStruct(x.shape, F32),
        scratch_shapes=[pltpu.VMEM((tm + 2 * QBLK, kv_w), BF16), pltpu.VMEM((tm, ATTN_WIDTH), F32)],
        compiler_params=pltpu.CompilerParams(
            dimension_semantics=("parallel", "arbitrary"), vmem_limit_bytes=VMEM_LIMIT),
    )(sink, q, kv, kv, kv, cache_k, cache_v, ga, u, u, u, gp, x, mods, *params)


def _rope_tables(seq):
    t = jnp.arange(seq)
    row = (t // GRID_W).astype(F32)
    col = (t % GRID_W).astype(F32)
    inv = ROPE_THETA ** (-(jnp.arange(0, AXIS_DIM, 2, dtype=F32) / AXIS_DIM))
    ang_r = row[:, None] * inv[None, :]
    ang_c = col[:, None] * inv[None, :]
    cos_h = jnp.concatenate([jnp.cos(ang_r)] * 2 + [jnp.cos(ang_c)] * 2, axis=1)
    sin_h = jnp.concatenate([-jnp.sin(ang_r), jnp.sin(ang_r), -jnp.sin(ang_c), jnp.sin(ang_c)], axis=1)
    return jnp.tile(cos_h, (1, LANES // HEAD_DIM)), jnp.tile(sin_h, (1, LANES // HEAD_DIM))


def kernel(x_prompt, x_sample, cache_k, cache_v, c, c_ctx, norm_w, w_ada, b_ada, w_in, sink, attn_norm,
           pool_norm, w_pool, pool_scale, w_out, final_norm):
    depth = w_in.shape[0]
    n_ctx_b, ctx_len, _ = x_prompt.shape
    n_lat_b, lat_len, _ = x_sample.shape
    assert n_lat_b + 1 <= 8

    cvec = jnp.concatenate([c_ctx[None, :], c, jnp.zeros((8 - 1 - n_lat_b, D_MODEL), F32)], axis=0)
    mods = _ada(cvec, w_ada, b_ada).reshape(depth, 8, 1, 3 * D_MODEL)

    w_in_b = w_in.astype(BF16)
    w_out_b = w_out.astype(BF16)
    w_pool_b = w_pool.astype(BF16)
    norm_w3 = norm_w.reshape(depth, 1, D_MODEL)
    params = (attn_norm.reshape(depth, 1, ATTN_WIDTH), pool_norm.reshape(depth, 1, POOL_WIDTH),
              pool_scale.reshape(depth, 1, POOL_WIDTH), w_pool_b, w_out_b, final_norm.reshape(1, D_MODEL))
    ck = cache_k.reshape(n_lat_b, depth, cache_k.shape[2], KV_WIDTH)
    cv = cache_v.reshape(n_lat_b, depth, cache_v.shape[2], KV_WIDTH)
    rope_tabs = _rope_tables(lat_len)

    xp, xs = x_prompt, x_sample
    new_k, new_v = [], []
    tm = 512
    for l in range(depth):
        final = l == depth - 1
        q, kv, ga, u, gp, k32, v32 = _proj(
            xp.reshape(1, n_ctx_b * ctx_len, D_MODEL), mods, l, lambda b: 0, norm_w3, w_in_b, None, True, tm)
        seq3 = lambda a: a.reshape(n_ctx_b, ctx_len, a.shape[-1])
        xp = _mix_ctx(xp, seq3(q), seq3(kv), seq3(ga), seq3(u), seq3(gp), mods, l, params, sink, final)
        new_k.append(seq3(k32))
        new_v.append(seq3(v32))

        q, kv, ga, u, gp = _proj(xs, mods, l, lambda b: b + 1, norm_w3, w_in_b, rope_tabs, False, tm)
        xs = _mix_lat(xs, q, kv, ga, u, gp, ck, cv, mods, l, params, sink, final, tm)

    kv_shape = (n_ctx_b, depth, ctx_len, KV_WIDTH // HEAD_DIM, HEAD_DIM)
    k_state = jnp.stack(new_k, axis=1).reshape(kv_shape)
    v_state = jnp.stack(new_v, axis=1).reshape(kv_shape)
    return (xp, xs, k_state, v_state)
```

```python
import functools

import jax
import jax.numpy as jnp
from jax import lax
from jax.experimental import pallas as pl
from jax.experimental.pallas import tpu as pltpu

F32 = jnp.float32
BF16 = jnp.bfloat16

D_MODEL = 1024
N_HEADS = 8
GQA = 4
HEAD_DIM = 64
ATTN_WIDTH = N_HEADS * HEAD_DIM
KV_WIDTH = 2 * HEAD_DIM
POOL_WIDTH = D_MODEL - ATTN_WIDTH
POOL_WINDOWS = (2, 4, 8, 16)
POOL_GROUP_W = POOL_WIDTH // len(POOL_WINDOWS)
IN_WIDTH = ATTN_WIDTH + 2 * KV_WIDTH + ATTN_WIDTH + 2 * POOL_WIDTH
GRID_W = 64
WINDOW = 128
QBLK = 128
AXIS_DIM = HEAD_DIM // 2
ROPE_THETA = 10000.0
EPS = 1e-6
NEG_INF = -1e30
LOG2E = 1.4426950408889634
PIPE_DEPTH = 2
assert WINDOW == QBLK
OUT_CHUNK = 256
LANES = 128
QZ_WIDTH = N_HEADS * LANES
POOL_HALO = 8
VMEM_LIMIT = 56 * 1024 * 1024

O_Q = 0
O_K = ATTN_WIDTH
O_V = O_K + KV_WIDTH
O_GA = O_V + KV_WIDTH
O_U = O_GA + ATTN_WIDTH
O_GP = O_U + POOL_WIDTH


def _rms(x, w):
    r = lax.rsqrt(jnp.mean(x * x, axis=-1, keepdims=True) + EPS)
    return (x * r) * w


def _silu(x):
    return x * jax.nn.sigmoid(x)


def _ada_kernel(c_ref, w_ref, b_ref, o_ref):
    a = _silu(c_ref[...])
    w = w_ref[...]
    a_hi = a.astype(BF16)
    a_lo = (a - a_hi.astype(F32)).astype(BF16)
    w_hi = w.astype(BF16)
    w_lo = (w - w_hi.astype(F32)).astype(BF16)
    lhs = jnp.concatenate([a_hi, a_lo], axis=0)
    r = (jnp.dot(lhs, w_hi, preferred_element_type=F32)
         + jnp.dot(lhs, w_lo, preferred_element_type=F32))
    o_ref[...] = r[:8] + r[8:] + b_ref[...]


def _ada(cvec, w_ada, b_ada):
    depth = w_ada.shape[0]
    tn = 1024
    return pl.pallas_call(
        _ada_kernel,
        grid=(depth, 3 * D_MODEL // tn),
        in_specs=[
            pl.BlockSpec((8, D_MODEL), lambda l, j: (0, 0)),
            pl.BlockSpec((None, D_MODEL, tn), lambda l, j: (l, 0, j)),
            pl.BlockSpec((None, 1, tn), lambda l, j: (l, 0, j)),
        ],
        out_specs=pl.BlockSpec((None, 8, tn), lambda l, j: (l, 0, j)),
        out_shape=jax.ShapeDtypeStruct((depth, 8, 3 * D_MODEL), F32),
        compiler_params=pltpu.CompilerParams(
            dimension_semantics=("parallel", "parallel"), vmem_limit_bytes=VMEM_LIMIT),
    )(cvec, w_ada, b_ada.reshape(depth, 1, 3 * D_MODEL))


def _rope(z, cos, sin):
    lane = lax.broadcasted_iota(jnp.int32, z.shape, 1)
    first = (lane % (2 * 16)) < 16
    partner = jnp.where(first, pltpu.roll(z, LANES - 16, 1), pltpu.roll(z, 16, 1))
    return z * cos + partner * sin


def _proj_kernel(*refs, rope, emit_kv):
    x_ref, mod_ref, nw_ref, w_ref = refs[:4]
    refs = refs[4:]
    if rope:
        cos_ref, sin_ref = refs[:2]
        refs = refs[2:]
    q_ref, kv_ref, ga_ref, u_ref, gp_ref = refs[:5]
    refs = refs[5:]

    x = x_ref[...]
    shift = mod_ref[:, 0:D_MODEL]
    scale = mod_ref[:, D_MODEL:2 * D_MODEL]
    h = _rms(x, nw_ref[...]) * (1.0 + scale) + shift
    hb = h.astype(BF16)

    def project(lo, width):
        return jnp.dot(hb, w_ref[:, lo:lo + width], preferred_element_type=F32)

    q = project(O_Q, ATTN_WIDTH)
    kv = project(O_K, 2 * KV_WIDTH)
    k, v = kv[:, :KV_WIDTH], kv[:, KV_WIDTH:]
    if emit_kv:
        k_ref, v_ref = refs
        k_ref[...] = k
        v_ref[...] = v
    if rope:
        cos = cos_ref[...]
        sin = sin_ref[...]
        q = jnp.concatenate(
            [_rope(q[:, j * LANES:(j + 1) * LANES], cos, sin) for j in range(ATTN_WIDTH // LANES)], axis=1)
        k = _rope(k, cos, sin)
    q = q * (HEAD_DIM ** -0.5 * LOG2E)
    low = lax.broadcasted_iota(jnp.int32, (q.shape[0], LANES), 1) < HEAD_DIM
    slots = []
    for pair in range(N_HEADS // 2):
        both = q[:, pair * LANES:(pair + 1) * LANES]
        slots += [jnp.where(low, both, 0.0), jnp.where(low, pltpu.roll(both, HEAD_DIM, 1), 0.0)]
    q_ref[...] = jnp.concatenate(slots, axis=1).astype(BF16)
    kv_ref[...] = jnp.concatenate(
        [k, pltpu.roll(k, HEAD_DIM, 1), v, pltpu.roll(v, HEAD_DIM, 1)], axis=1).astype(BF16)
    ga_ref[...] = project(O_GA, ATTN_WIDTH)
    u_ref[...] = project(O_U, POOL_WIDTH)
    gp_ref[...] = project(O_GP, POOL_WIDTH)


def _proj(x, mods, layer, row_of_batch, norm_w, w_in_b, rope_tabs, emit_kv, tm):
    nb, seq, _ = x.shape
    rope = rope_tabs is not None
    grid = (nb, seq // tm)
    tok = lambda width: pl.BlockSpec((None, tm, width), lambda b, t: (b, t, 0))
    in_specs = [
        tok(D_MODEL),
        pl.BlockSpec((None, None, 1, 3 * D_MODEL), lambda b, t: (layer, row_of_batch(b), 0, 0)),
        pl.BlockSpec((None, 1, D_MODEL), lambda b, t: (layer, 0, 0)),
        pl.BlockSpec((None, D_MODEL, IN_WIDTH), lambda b, t: (layer, 0, 0)),
    ]
    args = [x, mods, norm_w, w_in_b]
    if rope:
        in_specs += [pl.BlockSpec((tm, LANES), lambda b, t: (t, 0))] * 2
        args += list(rope_tabs)
    out_specs = [tok(QZ_WIDTH), tok(4 * KV_WIDTH), tok(ATTN_WIDTH), tok(POOL_WIDTH), tok(POOL_WIDTH)]
    out_shape = [
        jax.ShapeDtypeStruct((nb, seq, QZ_WIDTH), BF16),
        jax.ShapeDtypeStruct((nb, seq, 4 * KV_WIDTH), BF16),
        jax.ShapeDtypeStruct((nb, seq, ATTN_WIDTH), F32),
        jax.ShapeDtypeStruct((nb, seq, POOL_WIDTH), F32),
        jax.ShapeDtypeStruct((nb, seq, POOL_WIDTH), F32),
    ]
    if emit_kv:
        out_specs += [tok(KV_WIDTH), tok(KV_WIDTH)]
        out_shape += [jax.ShapeDtypeStruct((nb, seq, KV_WIDTH), F32)] * 2
    return pl.pallas_call(
        functools.partial(_proj_kernel, rope=rope, emit_kv=emit_kv),
        grid=grid, in_specs=in_specs, out_specs=out_specs, out_shape=out_shape,
        compiler_params=pltpu.CompilerParams(
            dimension_semantics=("parallel", "parallel"), vmem_limit_bytes=VMEM_LIMIT),
    )(*args)


def _softmax(s, biases, sk2):
    if biases:
        blocks = [s[:, j * LANES:(j + 1) * LANES] for j in range(s.shape[1] // LANES)]
        s = jnp.concatenate([b + biases[j] if j in biases else b for j, b in enumerate(blocks)], axis=1)
    m = jnp.maximum(jnp.max(s, axis=-1, keepdims=True), sk2)
    e = jnp.exp2(s - m)
    den = jnp.sum(e, axis=-1, keepdims=True) + jnp.exp2(sk2 - m)
    return e.astype(BF16), 1.0 / den


def _attend(blocks, sink2_of_head, store, fillers, after_block):
    tq = blocks[0][0].shape[0]
    low = lax.broadcasted_iota(jnp.int32, (tq, LANES), 1) < HEAD_DIM
    n_kv = N_HEADS // GQA
    tasks = [(bi, g) for bi in range(len(blocks)) for g in range(n_kv)]

    def score(task):
        bi, g = task
        qz, k, ksw = blocks[bi][:3]
        lhs = jnp.concatenate([qz[:, h * LANES:(h + 1) * LANES] for h in range(g * GQA, (g + 1) * GQA)], axis=0)
        return lax.dot_general(lhs, ksw if g else k, (((1,), (1,)), ((), ())), preferred_element_type=F32)

    pending = {i: score(tasks[i]) for i in range(min(PIPE_DEPTH, len(tasks)))}
    for i, (bi, g) in enumerate(tasks):
        if i + PIPE_DEPTH < len(tasks):
            pending[i + PIPE_DEPTH] = score(tasks[i + PIPE_DEPTH])
        _, _, _, v, vsw, biases = blocks[bi]
        s = pending.pop(i)
        es, scales = [], []
        for j in range(GQA):
            for filler in fillers.get(GQA * i + j, ()):
                filler()
            e, scale = _softmax(s[j * tq:(j + 1) * tq], biases, sink2_of_head(g * GQA + j))
            es.append(e)
            scales.append(scale)
        v_low, v_high = (vsw, v) if g else (v, vsw)
        o_even = jnp.dot(jnp.concatenate(es[0::2], axis=0), v_low, preferred_element_type=F32)
        o_odd = jnp.dot(jnp.concatenate(es[1::2], axis=0), v_high, preferred_element_type=F32)
        for p in range(GQA // 2):
            rows = slice(p * tq, (p + 1) * tq)
            store(bi, g * (GQA // 2) + p,
                  jnp.where(low, o_even[rows] * scales[2 * p], o_odd[rows] * scales[2 * p + 1]))
        if g == n_kv - 1:
            after_block(bi)


def _pool_group(g, u_ref, u_prev, u_next, pos, seq_len, wp_ref):
    w = POOL_WINDOWS[g]
    tm = u_ref.shape[0]
    sl = slice(g * POOL_GROUP_W, (g + 1) * POOL_GROUP_W)
    pad = jnp.zeros((POOL_HALO, POOL_GROUP_W), F32)
    rows = tm + 4 * POOL_HALO
    ug = u_ref[:, sl]
    ext = jnp.concatenate([pad, u_prev[:, sl], ug, u_next[:, sl], pad], axis=0)
    s = ext + pltpu.roll(ext, 1, 0)
    step = 1
    while 2 * step < w:
        s = pltpu.roll(s, step, 0) + pltpu.roll(s, rows - step, 0)
        step *= 2
    ssum = s[2 * POOL_HALO:2 * POOL_HALO + tm]
    lo = jnp.clip(pos - w // 2, 0, seq_len)
    hi = jnp.clip(pos + w // 2, 0, seq_len)
    pooled = ssum / (hi - lo).astype(F32) - ug
    return jnp.dot(pooled.astype(BF16), wp_ref[g], preferred_element_type=F32)


def _mix_tile(blocks, u_prev, u_next, pos, seq_len, layer, final, sink_ref, ga_ref, u_ref, gp_ref, x_ref,
              mod_ref, an_ref, pn_ref, ps_ref, wp_ref, wo_ref, fn_ref, o_ref, attn_ref, pool_ref, pcat_ref,
              acat_ref):
    tm = x_ref.shape[0]
    n_blocks = len(blocks)
    assert tm == n_blocks * QBLK
    gate = mod_ref[:, 2 * D_MODEL:]
    rows_of = lambda c: slice(c * QBLK, (c + 1) * QBLK)

    def pool_group(g):
        def run():
            sl = slice(g * POOL_GROUP_W, (g + 1) * POOL_GROUP_W)
            pool_ref[:, sl] = _pool_group(g, u_ref, u_prev, u_next, pos, seq_len, wp_ref) * ps_ref[:, sl]
        return run

    def pool_gate(c):
        def run():
            r = rows_of(c)
            pcat_ref[r, :] = (_rms(pool_ref[r, :], pn_ref[...]) * _silu(gp_ref[r, :])).astype(BF16)
        return run

    def pool_projection(j):
        def run():
            cols = slice(j * OUT_CHUNK, (j + 1) * OUT_CHUNK)
            y = jnp.dot(pcat_ref[...], wo_ref[ATTN_WIDTH:, cols], preferred_element_type=F32)
            o_ref[:, cols] = x_ref[:, cols] + gate[:, cols] * y
        return run

    def attn_gate(c):
        r = rows_of(c)
        acat_ref[r, :] = (_rms(attn_ref[r, :], an_ref[...]) * _silu(ga_ref[r, :])).astype(BF16)

    def store(c, pair, tile):
        attn_ref[rows_of(c), pair * LANES:(pair + 1) * LANES] = tile

    work = ([pool_group(g) for g in range(len(POOL_WINDOWS))] + [pool_gate(c) for c in range(n_blocks)]
            + [pool_projection(j) for j in range(D_MODEL // OUT_CHUNK)])
    n_tasks = n_blocks * N_HEADS
    fillers = {}
    for i, f in enumerate(work):
        fillers.setdefault(i * (n_tasks - 1) // len(work), []).append(f)
    _attend(blocks, lambda h: sink_ref[layer, h] * LOG2E, store, fillers, attn_gate)

    for j in range(D_MODEL // OUT_CHUNK):
        cols = slice(j * OUT_CHUNK, (j + 1) * OUT_CHUNK)
        y = jnp.dot(acat_ref[...], wo_ref[:ATTN_WIDTH, cols], preferred_element_type=F32)
        o_ref[:, cols] = o_ref[:, cols] + gate[:, cols] * y
    if final:
        o_ref[...] = _rms(o_ref[...], fn_ref[...])


def _mix_ctx_kernel(sink_ref, q_ref, kv_ref, ga_ref, u_ref, gp_ref, x_ref, mod_ref, an_ref, pn_ref,
                    ps_ref, wp_ref, wo_ref, fn_ref, o_ref, attn_ref, pool_ref, pcat_ref, acat_ref, *,
                    layer, final):
    tm = q_ref.shape[0]
    kv = kv_ref[...]
    k, ksw, v, vsw = (kv[:, i * KV_WIDTH:(i + 1) * KV_WIDTH] for i in range(4))
    pos = lax.broadcasted_iota(jnp.int32, (tm, 1), 0)
    halo = jnp.zeros((POOL_HALO, POOL_WIDTH), F32)
    blocks = [(q_ref[c * QBLK:(c + 1) * QBLK, :], k, ksw, v, vsw, None) for c in range(tm // QBLK)]
    _mix_tile(blocks, halo, halo, pos, tm, layer, final, sink_ref, ga_ref, u_ref, gp_ref, x_ref, mod_ref,
              an_ref, pn_ref, ps_ref, wp_ref, wo_ref, fn_ref, o_ref, attn_ref, pool_ref, pcat_ref, acat_ref)


def _mix_lat_kernel(sink_ref, q_ref, kvc_ref, kvp_ref, kvn_ref, ck_ref, cv_ref, ga_ref, u_ref, up_ref,
                    un_ref, gp_ref, x_ref, mod_ref, an_ref, pn_ref, ps_ref, wp_ref, wo_ref, fn_ref,
                    o_ref, kx_ref, attn_ref, pool_ref, pcat_ref, acat_ref, *, layer, final, seq_len):
    tm = q_ref.shape[0]
    t0 = pl.program_id(1) * tm
    kx_ref[0:QBLK, :] = kvp_ref[...]
    kx_ref[QBLK:QBLK + tm, :] = kvc_ref[...]
    kx_ref[QBLK + tm:, :] = kvn_ref[...]
    ck = ck_ref[...]
    cv = cv_ref[...]
    ckb, ckswb = ck.astype(BF16), pltpu.roll(ck, HEAD_DIM, 1).astype(BF16)
    cvb, cvswb = cv.astype(BF16), pltpu.roll(cv, HEAD_DIM, 1).astype(BF16)
    span = QBLK + 2 * WINDOW
    qi = lax.broadcasted_iota(jnp.int32, (QBLK, LANES), 0)
    kj = lax.broadcasted_iota(jnp.int32, (QBLK, LANES), 1)
    blocks = []
    for sub in range(tm // QBLK):
        band = kx_ref[sub * QBLK:sub * QBLK + span, :]
        k, ksw, v, vsw = (band[:, i * KV_WIDTH:(i + 1) * KV_WIDTH] for i in range(4))
        start = t0 + sub * QBLK
        biases = {
            0: jnp.where((kj >= qi) & (start - WINDOW >= 0), 0.0, NEG_INF),
            2: jnp.where((kj <= qi) & (start + QBLK + WINDOW <= seq_len), 0.0, NEG_INF),
        }
        blocks.append((q_ref[sub * QBLK:(sub + 1) * QBLK, :],
                       jnp.concatenate([k, ckb], axis=0), jnp.concatenate([ksw, ckswb], axis=0),
                       jnp.concatenate([v, cvb], axis=0), jnp.concatenate([vsw, cvswb], axis=0), biases))

    pos = t0 + lax.broadcasted_iota(jnp.int32, (tm, 1), 0)
    u_prev = jnp.where(t0 > 0, up_ref[...], 0.0)
    u_next = jnp.where(t0 + tm < seq_len, un_ref[...], 0.0)
    _mix_tile(blocks, u_prev, u_next, pos, seq_len, layer, final, sink_ref, ga_ref, u_ref, gp_ref, x_ref,
              mod_ref, an_ref, pn_ref, ps_ref, wp_ref, wo_ref, fn_ref, o_ref, attn_ref, pool_ref, pcat_ref,
              acat_ref)


def _mix_scratch(tm):
    return [pltpu.VMEM((tm, ATTN_WIDTH), F32), pltpu.VMEM((tm, POOL_WIDTH), F32),
            pltpu.VMEM((tm, POOL_WIDTH), BF16), pltpu.VMEM((tm, ATTN_WIDTH), BF16)]


def _param_specs(layer):
    return [
        pl.BlockSpec((None, 1, ATTN_WIDTH), lambda b, t: (layer, 0, 0)),
        pl.BlockSpec((None, 1, POOL_WIDTH), lambda b, t: (layer, 0, 0)),
        pl.BlockSpec((None, 1, POOL_WIDTH), lambda b, t: (layer, 0, 0)),
        pl.BlockSpec((None, len(POOL_WINDOWS), POOL_GROUP_W, POOL_GROUP_W), lambda b, t: (layer, 0, 0, 0)),
        pl.BlockSpec((None, D_MODEL, D_MODEL), lambda b, t: (layer, 0, 0)),
        pl.BlockSpec((1, D_MODEL), lambda b, t: (0, 0)),
    ]


def _mix_ctx(x, q, kv, ga, u, gp, mods, layer, params, sink, final):
    nb, tm, _ = x.shape
    tok = lambda width: pl.BlockSpec((None, tm, width), lambda b, t: (b, 0, 0))
    in_specs = [
        pl.BlockSpec(memory_space=pltpu.SMEM),
        tok(QZ_WIDTH), tok(4 * KV_WIDTH), tok(ATTN_WIDTH), tok(POOL_WIDTH), tok(POOL_WIDTH), tok(D_MODEL),
        pl.BlockSpec((None, None, 1, 3 * D_MODEL), lambda b, t: (layer, 0, 0, 0)),
    ] + _param_specs(layer)
    return pl.pallas_call(
        functools.partial(_mix_ctx_kernel, layer=layer, final=final),
        grid=(nb, 1), in_specs=in_specs, out_specs=tok(D_MODEL),
        out_shape=jax.ShapeDtypeStruct(x.shape, F32),
        scratch_shapes=_mix_scratch(tm),
        compiler_params=pltpu.CompilerParams(
            dimension_semantics=("parallel", "arbitrary"), vmem_limit_bytes=VMEM_LIMIT),
    )(sink, q, kv, ga, u, gp, x, mods, *params)


def _mix_lat(x, q, kv, ga, u, gp, cache_k, cache_v, mods, layer, params, sink, final, tm):
    nb, seq, _ = x.shape
    nq = seq // QBLK
    nh = seq // POOL_HALO
    tok = lambda width: pl.BlockSpec((None, tm, width), lambda b, t: (b, t, 0))
    kv_w = 4 * KV_WIDTH
    n_ctx = cache_k.shape[2]
    in_specs = [
        pl.BlockSpec(memory_space=pltpu.SMEM),
        tok(QZ_WIDTH),
        tok(kv_w),
        pl.BlockSpec((None, QBLK, kv_w), lambda b, t: (b, jnp.maximum(t * (tm // QBLK) - 1, 0), 0)),
        pl.BlockSpec((None, QBLK, kv_w), lambda b, t: (b, jnp.minimum((t + 1) * (tm // QBLK), nq - 1), 0)),
        pl.BlockSpec((None, None, n_ctx, KV_WIDTH), lambda b, t: (b, layer, 0, 0)),
        pl.BlockSpec((None, None, n_ctx, KV_WIDTH), lambda b, t: (b, layer, 0, 0)),
        tok(ATTN_WIDTH),
        tok(POOL_WIDTH),
        pl.BlockSpec((None, POOL_HALO, POOL_WIDTH),
                     lambda b, t: (b, jnp.maximum(t * (tm // POOL_HALO) - 1, 0), 0)),
        pl.BlockSpec((None, POOL_HALO, POOL_WIDTH),
                     lambda b, t: (b, jnp.minimum((t + 1) * (tm // POOL_HALO), nh - 1), 0)),
        tok(POOL_WIDTH),
        tok(D_MODEL),
        pl.BlockSpec((None, None, 1, 3 * D_MODEL), lambda b, t: (layer, b + 1, 0, 0)),
    ] + _param_specs(layer)
    return pl.pallas_call(
        functools.partial(_mix_lat_kernel, layer=layer, final=final, seq_len=seq),
        grid=(nb, seq // tm), in_specs=in_specs, out_specs=tok(D_MODEL),
        out_shape=jax.ShapeDtypeStruct(x.shape, F32),
        scratch_shapes=[pltpu.VMEM((tm + 2 * QBLK, kv_w), BF16)] + _mix_scratch(tm),
        compiler_params=pltpu.CompilerParams(
            dimension_semantics=("parallel", "arbitrary"), vmem_limit_bytes=VMEM_LIMIT),
    )(sink, q, kv, kv, kv, cache_k, cache_v, ga, u, u, u, gp, x, mods, *params)


def _rope_tables(seq):
    t = jnp.arange(seq)
    row = (t // GRID_W).astype(F32)
    col = (t % GRID_W).astype(F32)
    inv = ROPE_THETA ** (-(jnp.arange(0, AXIS_DIM, 2, dtype=F32) / AXIS_DIM))
    ang_r = row[:, None] * inv[None, :]
    ang_c = col[:, None] * inv[None, :]
    cos_h = jnp.concatenate([jnp.cos(ang_r)] * 2 + [jnp.cos(ang_c)] * 2, axis=1)
    sin_h = jnp.concatenate([-jnp.sin(ang_r), jnp.sin(ang_r), -jnp.sin(ang_c), jnp.sin(ang_c)], axis=1)
    return jnp.tile(cos_h, (1, LANES // HEAD_DIM)), jnp.tile(sin_h, (1, LANES // HEAD_DIM))


def kernel(x_prompt, x_sample, cache_k, cache_v, c, c_ctx, norm_w, w_ada, b_ada, w_in, sink, attn_norm,
           pool_norm, w_pool, pool_scale, w_out, final_norm):
    depth = w_in.shape[0]
    n_ctx_b, ctx_len, _ = x_prompt.shape
    n_lat_b, lat_len, _ = x_sample.shape
    assert n_lat_b + 1 <= 8

    cvec = jnp.concatenate([c_ctx[None, :], c, jnp.zeros((8 - 1 - n_lat_b, D_MODEL), F32)], axis=0)
    mods = _ada(cvec, w_ada, b_ada).reshape(depth, 8, 1, 3 * D_MODEL)

    w_in_b = w_in.astype(BF16)
    w_out_b = w_out.astype(BF16)
    w_pool_b = w_pool.astype(BF16)
    norm_w3 = norm_w.reshape(depth, 1, D_MODEL)
    params = (attn_norm.reshape(depth, 1, ATTN_WIDTH), pool_norm.reshape(depth, 1, POOL_WIDTH),
              pool_scale.reshape(depth, 1, POOL_WIDTH), w_pool_b, w_out_b, final_norm.reshape(1, D_MODEL))
    ck = cache_k.reshape(n_lat_b, depth, cache_k.shape[2], KV_WIDTH)
    cv = cache_v.reshape(n_lat_b, depth, cache_v.shape[2], KV_WIDTH)
    rope_tabs = _rope_tables(lat_len)

    xp, xs = x_prompt, x_sample
    new_k, new_v = [], []
    tm = 512
    for l in range(depth):
        final = l == depth - 1
        q, kv, ga, u, gp, k32, v32 = _proj(
            xp.reshape(1, n_ctx_b * ctx_len, D_MODEL), mods, l, lambda b: 0, norm_w3, w_in_b, None, True, tm)
        seq3 = lambda a: a.reshape(n_ctx_b, ctx_len, a.shape[-1])
        xp = _mix_ctx(xp, seq3(q), seq3(kv), seq3(ga), seq3(u), seq3(gp), mods, l, params, sink, final)
        new_k.append(seq3(k32))
        new_v.append(seq3(v32))

        q, kv, ga, u, gp = _proj(xs, mods, l, lambda b: b + 1, norm_w3, w_in_b, rope_tabs, False, tm)
        xs = _mix_lat(xs, q, kv, ga, u, gp, ck, cv, mods, l, params, sink, final, tm)

    kv_shape = (n_ctx_b, depth, ctx_len, KV_WIDTH // HEAD_DIM, HEAD_DIM)
    k_state = jnp.stack(new_k, axis=1).reshape(kv_shape)
    v_state = jnp.stack(new_v, axis=1).reshape(kv_shape)
    return (xp, xs, k_state, v_state)
```

```python
import functools

import jax
import jax.numpy as jnp
from jax import lax
from jax.experimental import pallas as pl
from jax.experimental.pallas import tpu as pltpu

F32 = jnp.float32
BF16 = jnp.bfloat16

D_MODEL = 1024
N_HEADS = 8
GQA = 4
HEAD_DIM = 64
ATTN_WIDTH = N_HEADS * HEAD_DIM
KV_WIDTH = 2 * HEAD_DIM
POOL_WIDTH = D_MODEL - ATTN_WIDTH
POOL_WINDOWS = (2, 4, 8, 16)
POOL_GROUP_W = POOL_WIDTH // len(POOL_WINDOWS)
IN_WIDTH = ATTN_WIDTH + 2 * KV_WIDTH + ATTN_WIDTH + 2 * POOL_WIDTH
GRID_W = 64
WINDOW = 128
QBLK = 128
AXIS_DIM = HEAD_DIM // 2
ROPE_THETA = 10000.0
EPS = 1e-6
NEG_INF = -1e30
LOG2E = 1.4426950408889634
PIPE_DEPTH = 2
assert WINDOW == QBLK
MXU_TILE = 256
LANES = 128
QZ_WIDTH = N_HEADS * LANES
KV4_WIDTH = 4 * KV_WIDTH
POOL_HALO = 8
TILE = 512
VMEM_LIMIT = 56 * 1024 * 1024

O_Q = 0
O_K = ATTN_WIDTH
O_GA = O_K + 2 * KV_WIDTH
O_U = O_GA + ATTN_WIDTH
O_GP = O_U + POOL_WIDTH


def _rms(x, w):
    r = lax.rsqrt(jnp.mean(x * x, axis=-1, keepdims=True) + EPS)
    return (x * r) * w


def _silu(x):
    return x * jax.nn.sigmoid(x)


def _ada_kernel(c_ref, w_ref, b_ref, o_ref):
    a = _silu(c_ref[...])
    w = w_ref[...]
    a_hi = a.astype(BF16)
    a_lo = (a - a_hi.astype(F32)).astype(BF16)
    w_hi = w.astype(BF16)
    w_lo = (w - w_hi.astype(F32)).astype(BF16)
    lhs = jnp.concatenate([a_hi, a_lo], axis=0)
    r = (jnp.dot(lhs, w_hi, preferred_element_type=F32)
         + jnp.dot(lhs, w_lo, preferred_element_type=F32))
    o_ref[...] = r[:8] + r[8:] + b_ref[...]


def _ada(cvec, w_ada, b_ada):
    depth = w_ada.shape[0]
    tn = 1024
    return pl.pallas_call(
        _ada_kernel,
        grid=(depth, 3 * D_MODEL // tn),
        in_specs=[
            pl.BlockSpec((8, D_MODEL), lambda l, j: (0, 0)),
            pl.BlockSpec((None, D_MODEL, tn), lambda l, j: (l, 0, j)),
            pl.BlockSpec((None, 1, tn), lambda l, j: (l, 0, j)),
        ],
        out_specs=pl.BlockSpec((None, 8, tn), lambda l, j: (l, 0, j)),
        out_shape=jax.ShapeDtypeStruct((depth, 8, 3 * D_MODEL), F32),
        compiler_params=pltpu.CompilerParams(
            dimension_semantics=("parallel", "parallel"), vmem_limit_bytes=VMEM_LIMIT),
    )(cvec, w_ada, b_ada.reshape(depth, 1, 3 * D_MODEL))


def _rope(z, cos, sin):
    lane = lax.broadcasted_iota(jnp.int32, z.shape, 1)
    first = (lane % (2 * 16)) < 16
    partner = jnp.where(first, pltpu.roll(z, LANES - 16, 1), pltpu.roll(z, 16, 1))
    return z * cos + partner * sin


def _softmax(s, biases, sk2):
    if biases:
        blocks = [s[:, j * LANES:(j + 1) * LANES] for j in range(s.shape[1] // LANES)]
        s = jnp.concatenate([b + biases[j] if j in biases else b for j, b in enumerate(blocks)], axis=1)
    m = jnp.maximum(jnp.max(s, axis=-1, keepdims=True), sk2)
    e = jnp.exp2(s - m)
    den = jnp.sum(e, axis=-1, keepdims=True) + jnp.exp2(sk2 - m)
    return e.astype(BF16), 1.0 / den


def _attend(get_block, n_blocks, sink2_of_head, store, fillers, after_block):
    low = lax.broadcasted_iota(jnp.int32, (QBLK, LANES), 1) < HEAD_DIM
    n_kv = N_HEADS // GQA
    tasks = [(b, g) for b in range(n_blocks) for g in range(n_kv)]

    def score(task):
        b, g = task
        qz, k, ksw = get_block(b)[:3]
        lhs = jnp.concatenate([qz[:, h * LANES:(h + 1) * LANES] for h in range(g * GQA, (g + 1) * GQA)], axis=0)
        return lax.dot_general(lhs, ksw if g else k, (((1,), (1,)), ((), ())), preferred_element_type=F32)

    pending = {i: score(tasks[i]) for i in range(min(PIPE_DEPTH, len(tasks)))}
    for i, (b, g) in enumerate(tasks):
        if i + PIPE_DEPTH < len(tasks):
            pending[i + PIPE_DEPTH] = score(tasks[i + PIPE_DEPTH])
        s = pending.pop(i)
        es, scales = [], []
        for j in range(GQA):
            for filler in fillers.get(GQA * i + j, ()):
                filler()
            if j == 0:
                _, _, _, v, vsw, biases = get_block(b)
            e, scale = _softmax(s[j * QBLK:(j + 1) * QBLK], biases, sink2_of_head(g * GQA + j))
            es.append(e)
            scales.append(scale)
        v_low, v_high = (vsw, v) if g else (v, vsw)
        o_even = jnp.dot(jnp.concatenate(es[0::2], axis=0), v_low, preferred_element_type=F32)
        o_odd = jnp.dot(jnp.concatenate(es[1::2], axis=0), v_high, preferred_element_type=F32)
        for p in range(GQA // 2):
            rows = slice(p * QBLK, (p + 1) * QBLK)
            store(b, g * (GQA // 2) + p,
                  jnp.where(low, o_even[rows] * scales[2 * p], o_odd[rows] * scales[2 * p + 1]))
        if g == n_kv - 1:
            after_block(b)


def _pool_group(g, ug, u_prev, u_next, pos, seq_len, wp_ref):
    w = POOL_WINDOWS[g]
    n = ug.shape[0]
    pad = jnp.zeros((POOL_HALO, POOL_GROUP_W), F32)
    rows = n + 4 * POOL_HALO
    ext = jnp.concatenate([pad, u_prev, ug, u_next, pad], axis=0)
    s = ext + pltpu.roll(ext, 1, 0)
    step = 1
    while 2 * step < w:
        s = pltpu.roll(s, step, 0) + pltpu.roll(s, rows - step, 0)
        step *= 2
    ssum = s[2 * POOL_HALO:2 * POOL_HALO + n]
    lo = jnp.clip(pos - w // 2, 0, seq_len)
    hi = jnp.clip(pos + w // 2, 0, seq_len)
    pooled = ssum / (hi - lo).astype(F32) - ug
    return jnp.dot(pooled.astype(BF16), wp_ref[g], preferred_element_type=F32)


def _layer_kernel(*refs, rope, band, emit_kv, layer, final, seq_len, tiles_per_seq):
    it = iter(refs)
    take = lambda n: [next(it) for _ in range(n)]
    sink_ref, xn_ref, xr_ref, modn_ref, modr_ref, nw_ref, win_ref = take(7)
    cos_ref, sin_ref = take(2) if rope else (None, None)
    ck_ref, cv_ref = take(2) if band else (None, None)
    an_ref, pn_ref, ps_ref, wp_ref, wo_ref, fn_ref = take(6)
    (o_ref,) = take(1)
    k32_ref, v32_ref = take(2) if emit_kv else (None, None)
    hb_ref, qz_ref, kv_ref, ga_ref, u_ref, gp_ref = take(6)
    kvp_ref, up_ref = take(2) if band else (None, None)
    attn_ref, pool_ref, pcat_ref, acat_ref = take(4)

    tm = xn_ref.shape[0]
    n_blocks = tm // QBLK
    j = pl.program_id(0)
    new = j % 2
    cur = 1 - new
    t0 = (jnp.maximum(j - 1, 0) % tiles_per_seq) * tm

    @pl.when(j == 0)
    def _():
        qz_ref[1] = jnp.zeros(qz_ref.shape[1:], BF16)
        ga_ref[1] = jnp.zeros(ga_ref.shape[1:], F32)
        gp_ref[1] = jnp.zeros(gp_ref.shape[1:], F32)
        kv_ref[...] = jnp.zeros(kv_ref.shape, BF16)
        u_ref[...] = jnp.zeros(u_ref.shape, F32)

    qz_cur, kv_cur, ga_cur, u_cur, gp_cur = (r.at[cur] for r in (qz_ref, kv_ref, ga_ref, u_ref, gp_ref))
    qz_new, kv_new, ga_new, u_new, gp_new = (r.at[new] for r in (qz_ref, kv_ref, ga_ref, u_ref, gp_ref))

    if band:
        kvp_ref[...] = kv_new[tm - QBLK:, :]
        up_ref[...] = u_new[tm - POOL_HALO:, :]
        ck = ck_ref[...]
        cv = cv_ref[...]
        cache = (ck.astype(BF16), pltpu.roll(ck, HEAD_DIM, 1).astype(BF16),
                 cv.astype(BF16), pltpu.roll(cv, HEAD_DIM, 1).astype(BF16))


    def normalise():
        h = _rms(xn_ref[...], nw_ref[...]) * (1.0 + modn_ref[:, D_MODEL:2 * D_MODEL]) + modn_ref[:, 0:D_MODEL]
        hb_ref[...] = h.astype(BF16)

    def project(lo):
        return jnp.dot(hb_ref[...], win_ref[:, lo:lo + MXU_TILE], preferred_element_type=F32)

    def project_kv():
        kv = project(O_K)
        k, v = kv[:, :KV_WIDTH], kv[:, KV_WIDTH:]
        if emit_kv:
            k32_ref[...] = k
            v32_ref[...] = v
        if rope:
            k = _rope(k, cos_ref[...], sin_ref[...])
        kv_new[...] = jnp.concatenate(
            [k, pltpu.roll(k, HEAD_DIM, 1), v, pltpu.roll(v, HEAD_DIM, 1)], axis=1).astype(BF16)

    def project_q(c):
        def run():
            q = project(O_Q + c * MXU_TILE)
            low = lax.broadcasted_iota(jnp.int32, (tm, LANES), 1) < HEAD_DIM
            slots = []
            for p in range(MXU_TILE // LANES):
                both = q[:, p * LANES:(p + 1) * LANES]
                if rope:
                    both = _rope(both, cos_ref[...], sin_ref[...])
                both = both * (HEAD_DIM ** -0.5 * LOG2E)
                slots += [jnp.where(low, both, 0.0), jnp.where(low, pltpu.roll(both, HEAD_DIM, 1), 0.0)]
            width = 2 * MXU_TILE
            qz_new[:, c * width:(c + 1) * width] = jnp.concatenate(slots, axis=1).astype(BF16)
        return run

    def project_plain(dst, lo, c):
        def run():
            dst[:, c * MXU_TILE:(c + 1) * MXU_TILE] = project(lo + c * MXU_TILE)
        return run


    gate = modr_ref[:, 2 * D_MODEL:]
    rows_of = lambda c: slice(c * QBLK, (c + 1) * QBLK)
    kv_parts = lambda rows: [rows[:, i * KV_WIDTH:(i + 1) * KV_WIDTH] for i in range(4)]

    def get_block(c):
        qz = qz_cur[rows_of(c), :]
        if not band:
            per_seq = seq_len // QBLK
            first = (c // per_seq) * seq_len
            return (qz, *kv_parts(kv_cur[first:first + seq_len, :]), None)
        if c == 0:
            rows = jnp.concatenate([kvp_ref[...], kv_cur[0:2 * QBLK, :]], axis=0)
        elif c == n_blocks - 1:
            rows = jnp.concatenate([kv_cur[tm - 2 * QBLK:, :], kv_new[0:QBLK, :]], axis=0)
        else:
            rows = kv_cur[(c - 1) * QBLK:(c + 2) * QBLK, :]
        qi = lax.broadcasted_iota(jnp.int32, (QBLK, LANES), 0)
        kj = lax.broadcasted_iota(jnp.int32, (QBLK, LANES), 1)
        start = t0 + c * QBLK
        biases = {
            0: jnp.where((kj >= qi) & (start - WINDOW >= 0), 0.0, NEG_INF),
            2: jnp.where((kj <= qi) & (start + QBLK + WINDOW <= seq_len), 0.0, NEG_INF),
        }
        return (qz, *(jnp.concatenate([a, b], axis=0) for a, b in zip(kv_parts(rows), cache)), biases)

    def pool_group(g, first, n):
        def run():
            sl = slice(g * POOL_GROUP_W, (g + 1) * POOL_GROUP_W)
            if band:
                pos = t0 + first + lax.broadcasted_iota(jnp.int32, (n, 1), 0)
                u_prev = jnp.where(t0 > 0, up_ref[:, sl], 0.0)
                u_next = jnp.where(t0 + tm < seq_len, u_new[0:POOL_HALO, sl], 0.0)
            else:
                pos = lax.broadcasted_iota(jnp.int32, (n, 1), 0)
                u_prev = u_next = jnp.zeros((POOL_HALO, POOL_GROUP_W), F32)
            y = _pool_group(g, u_cur[first:first + n, sl], u_prev, u_next, pos, seq_len, wp_ref)
            pool_ref[first:first + n, sl] = y * ps_ref[:, sl]
        return run

    def pool_gate(c):
        def run():
            r = rows_of(c)
            pcat_ref[r, :] = (_rms(pool_ref[r, :], pn_ref[...]) * _silu(gp_cur[r, :])).astype(BF16)
        return run

    def pool_projection(c):
        def run():
            cols = slice(c * MXU_TILE, (c + 1) * MXU_TILE)
            y = jnp.dot(pcat_ref[...], wo_ref[ATTN_WIDTH:, cols], preferred_element_type=F32)
            o_ref[:, cols] = xr_ref[:, cols] + gate[:, cols] * y
        return run

    def attn_gate(c):
        r = rows_of(c)
        acat_ref[r, :] = (_rms(attn_ref[r, :], an_ref[...]) * _silu(ga_cur[r, :])).astype(BF16)

    def store(c, pair, tile):
        attn_ref[rows_of(c), pair * LANES:(pair + 1) * LANES] = tile

    pool_rows = [(0, tm)] if band else [(s * seq_len, seq_len) for s in range(tm // seq_len)]
    halves = range(ATTN_WIDTH // MXU_TILE)
    work = ([normalise, project_kv] + [project_plain(u_new, O_U, c) for c in halves]
            + [pool_group(g, first, n) for first, n in pool_rows for g in range(len(POOL_WINDOWS))]
            + [project_plain(gp_new, O_GP, c) for c in halves]
            + [pool_gate(c) for c in range(n_blocks)]
            + [project_q(c) for c in halves] + [project_plain(ga_new, O_GA, c) for c in halves]
            + [pool_projection(c) for c in range(D_MODEL // MXU_TILE)])
    n_slots = n_blocks * N_HEADS
    fillers = {}
    for i, f in enumerate(work):
        fillers.setdefault(i * (n_slots - 1) // len(work), []).append(f)
    _attend(get_block, n_blocks, lambda h: sink_ref[layer, h] * LOG2E, store, fillers, attn_gate)

    for c in range(D_MODEL // MXU_TILE):
        cols = slice(c * MXU_TILE, (c + 1) * MXU_TILE)
        y = jnp.dot(acat_ref[...], wo_ref[:ATTN_WIDTH, cols], preferred_element_type=F32)
        o_ref[:, cols] = o_ref[:, cols] + gate[:, cols] * y
    if final:
        o_ref[...] = _rms(o_ref[...], fn_ref[...])


def _layer(x, mods, layer, row_of_tile, norm_w, w_in_b, rope_tabs, cache, params, sink, seq_len, final,
           emit_kv):
    n_tiles, tm, _ = x.shape
    rope = rope_tabs is not None
    band = cache is not None
    tiles_per_seq = max(seq_len // tm, 1)
    seq_in_tile = seq_len if seq_len < tm else tm
    assert tm % seq_in_tile == 0 and (seq_len <= tm or seq_len % tm == 0)
    proj_tile = lambda j: jnp.minimum(j, n_tiles - 1)
    mix_tile = lambda j: jnp.maximum(j - 1, 0)
    whole = lambda shape: pl.BlockSpec(shape, lambda j: (layer,) + (0,) * (len(shape) - 1))
    mod_spec = lambda tile_of: pl.BlockSpec(
        (None, None, 1, 3 * D_MODEL), lambda j: (layer, row_of_tile(tile_of(j)), 0, 0))
    in_specs = [
        pl.BlockSpec(memory_space=pltpu.SMEM),
        pl.BlockSpec((None, tm, D_MODEL), lambda j: (proj_tile(j), 0, 0)),
        pl.BlockSpec((None, tm, D_MODEL), lambda j: (mix_tile(j), 0, 0)),
        mod_spec(proj_tile), mod_spec(mix_tile),
        whole((None, 1, D_MODEL)),
        whole((None, D_MODEL, IN_WIDTH)),
    ]
    args = [sink, x, x, mods, mods, norm_w, w_in_b]
    if rope:
        in_specs += [pl.BlockSpec((tm, LANES), lambda j: (proj_tile(j) % tiles_per_seq, 0))] * 2
        args += list(rope_tabs)
    if band:
        n_ctx = cache[0].shape[2]
        in_specs += [pl.BlockSpec((None, None, n_ctx, KV_WIDTH),
                                  lambda j: (mix_tile(j) // tiles_per_seq, layer, 0, 0))] * 2
        args += list(cache)
    in_specs += [
        whole((None, 1, ATTN_WIDTH)), whole((None, 1, POOL_WIDTH)), whole((None, 1, POOL_WIDTH)),
        whole((None, len(POOL_WINDOWS), POOL_GROUP_W, POOL_GROUP_W)),
        whole((None, D_MODEL, D_MODEL)),
        pl.BlockSpec((1, D_MODEL), lambda j: (0, 0)),
    ]
    args += list(params)
    out_specs = [pl.BlockSpec((None, tm, D_MODEL), lambda j: (mix_tile(j), 0, 0))]
    out_shape = [jax.ShapeDtypeStruct(x.shape, F32)]
    if emit_kv:
        out_specs += [pl.BlockSpec((None, tm, KV_WIDTH), lambda j: (proj_tile(j), 0, 0))] * 2
        out_shape += [jax.ShapeDtypeStruct((n_tiles, tm, KV_WIDTH), F32)] * 2
    scratch = [pltpu.VMEM((tm, D_MODEL), BF16),
               pltpu.VMEM((2, tm, QZ_WIDTH), BF16), pltpu.VMEM((2, tm, KV4_WIDTH), BF16),
               pltpu.VMEM((2, tm, ATTN_WIDTH), F32), pltpu.VMEM((2, tm, POOL_WIDTH), F32),
               pltpu.VMEM((2, tm, POOL_WIDTH), F32)]
    if band:
        scratch += [pltpu.VMEM((QBLK, KV4_WIDTH), BF16), pltpu.VMEM((POOL_HALO, POOL_WIDTH), F32)]
    scratch += [pltpu.VMEM((tm, ATTN_WIDTH), F32), pltpu.VMEM((tm, POOL_WIDTH), F32),
                pltpu.VMEM((tm, POOL_WIDTH), BF16), pltpu.VMEM((tm, ATTN_WIDTH), BF16)]
    return pl.pallas_call(
        functools.partial(_layer_kernel, rope=rope, band=band, emit_kv=emit_kv, layer=layer, final=final,
                          seq_len=seq_len, tiles_per_seq=tiles_per_seq),
        grid=(n_tiles + 1,), in_specs=in_specs, out_specs=out_specs, out_shape=out_shape,
        scratch_shapes=scratch,
        compiler_params=pltpu.CompilerParams(dimension_semantics=("arbitrary",), vmem_limit_bytes=VMEM_LIMIT),
    )(*args)


def _rope_tables(seq):
    t = jnp.arange(seq)
    row = (t // GRID_W).astype(F32)
    col = (t % GRID_W).astype(F32)
    inv = ROPE_THETA ** (-(jnp.arange(0, AXIS_DIM, 2, dtype=F32) / AXIS_DIM))
    ang_r = row[:, None] * inv[None, :]
    ang_c = col[:, None] * inv[None, :]
    cos_h = jnp.concatenate([jnp.cos(ang_r)] * 2 + [jnp.cos(ang_c)] * 2, axis=1)
    sin_h = jnp.concatenate([-jnp.sin(ang_r), jnp.sin(ang_r), -jnp.sin(ang_c), jnp.sin(ang_c)], axis=1)
    return jnp.tile(cos_h, (1, LANES // HEAD_DIM)), jnp.tile(sin_h, (1, LANES // HEAD_DIM))


def kernel(x_prompt, x_sample, cache_k, cache_v, c, c_ctx, norm_w, w_ada, b_ada, w_in, sink, attn_norm,
           pool_norm, w_pool, pool_scale, w_out, final_norm):
    depth = w_in.shape[0]
    n_ctx_b, ctx_len, _ = x_prompt.shape
    n_lat_b, lat_len, _ = x_sample.shape
    assert n_lat_b + 1 <= 8

    cvec = jnp.concatenate([c_ctx[None, :], c, jnp.zeros((8 - 1 - n_lat_b, D_MODEL), F32)], axis=0)
    mods = _ada(cvec, w_ada, b_ada).reshape(depth, 8, 1, 3 * D_MODEL)

    w_in_b = w_in.astype(BF16)
    norm_w3 = norm_w.reshape(depth, 1, D_MODEL)
    params = (attn_norm.reshape(depth, 1, ATTN_WIDTH), pool_norm.reshape(depth, 1, POOL_WIDTH),
              pool_scale.reshape(depth, 1, POOL_WIDTH), w_pool.astype(BF16), w_out.astype(BF16),
              final_norm.reshape(1, D_MODEL))
    cache = (cache_k.reshape(n_lat_b, depth, cache_k.shape[2], KV_WIDTH),
             cache_v.reshape(n_lat_b, depth, cache_v.shape[2], KV_WIDTH))
    rope_tabs = _rope_tables(lat_len)
    lat_tiles_per_seq = lat_len // TILE

    xp = x_prompt.reshape(n_ctx_b * ctx_len // TILE, TILE, D_MODEL)
    xs = x_sample.reshape(n_lat_b * lat_len // TILE, TILE, D_MODEL)
    new_k, new_v = [], []
    for l in range(depth):
        final = l == depth - 1
        xp, k32, v32 = _layer(xp, mods, l, lambda t: 0, norm_w3, w_in_b, None, None, params, sink,
                              ctx_len, final, True)
        new_k.append(k32.reshape(n_ctx_b, ctx_len, KV_WIDTH))
        new_v.append(v32.reshape(n_ctx_b, ctx_len, KV_WIDTH))
        (xs,) = _layer(xs, mods, l, lambda t: 1 + t // lat_tiles_per_seq, norm_w3, w_in_b, rope_tabs, cache,
                       params, sink, lat_len, final, False)

    kv_shape = (n_ctx_b, depth, ctx_len, KV_WIDTH // HEAD_DIM, HEAD_DIM)
    k_state = jnp.stack(new_k, axis=1).reshape(kv_shape)
    v_state = jnp.stack(new_v, axis=1).reshape(kv_shape)
    return (xp.reshape(x_prompt.shape), xs.reshape(x_sample.shape), k_state, v_state)
```

```python
import functools

import jax
import jax.numpy as jnp
from jax import lax
from jax.experimental import pallas as pl
from jax.experimental.pallas import tpu as pltpu

F32 = jnp.float32
BF16 = jnp.bfloat16

D_MODEL = 1024
N_HEADS = 8
GQA = 4
HEAD_DIM = 64
ATTN_WIDTH = N_HEADS * HEAD_DIM
KV_WIDTH = 2 * HEAD_DIM
POOL_WIDTH = D_MODEL - ATTN_WIDTH
POOL_WINDOWS = (2, 4, 8, 16)
POOL_GROUP_W = POOL_WIDTH // len(POOL_WINDOWS)
IN_WIDTH = ATTN_WIDTH + 2 * KV_WIDTH + ATTN_WIDTH + 2 * POOL_WIDTH
GRID_W = 64
WINDOW = 128
QBLK = 128
AXIS_DIM = HEAD_DIM // 2
ROPE_THETA = 10000.0
EPS = 1e-6
NEG_INF = -1e30
LOG2E = 1.4426950408889634
PIPE_DEPTH = 2
assert WINDOW == QBLK
MXU_TILE = 256
LANES = 128
QZ_WIDTH = N_HEADS * LANES
KV4_WIDTH = 4 * KV_WIDTH
POOL_HALO = 8
TILE = 512
VMEM_LIMIT = 56 * 1024 * 1024

O_Q = 0
O_K = ATTN_WIDTH
O_GA = O_K + 2 * KV_WIDTH
O_U = O_GA + ATTN_WIDTH
O_GP = O_U + POOL_WIDTH


def _rms(x, w):
    r = lax.rsqrt(jnp.mean(x * x, axis=-1, keepdims=True) + EPS)
    return (x * r) * w


def _silu(x):
    return x * jax.nn.sigmoid(x)


def _ada_kernel(c_ref, w_ref, b_ref, o_ref):
    a = _silu(c_ref[...])
    w = w_ref[...]
    a_hi = a.astype(BF16)
    a_lo = (a - a_hi.astype(F32)).astype(BF16)
    w_hi = w.astype(BF16)
    w_lo = (w - w_hi.astype(F32)).astype(BF16)
    lhs = jnp.concatenate([a_hi, a_lo], axis=0)
    r = (jnp.dot(lhs, w_hi, preferred_element_type=F32)
         + jnp.dot(lhs, w_lo, preferred_element_type=F32))
    o_ref[...] = r[:8] + r[8:] + b_ref[...]


def _ada(cvec, w_ada, b_ada):
    depth = w_ada.shape[0]
    tn = 1024
    return pl.pallas_call(
        _ada_kernel,
        grid=(depth, 3 * D_MODEL // tn),
        in_specs=[
            pl.BlockSpec((8, D_MODEL), lambda l, j: (0, 0)),
            pl.BlockSpec((None, D_MODEL, tn), lambda l, j: (l, 0, j)),
            pl.BlockSpec((None, 1, tn), lambda l, j: (l, 0, j)),
        ],
        out_specs=pl.BlockSpec((None, 8, tn), lambda l, j: (l, 0, j)),
        out_shape=jax.ShapeDtypeStruct((depth, 8, 3 * D_MODEL), F32),
        compiler_params=pltpu.CompilerParams(
            dimension_semantics=("parallel", "parallel"), vmem_limit_bytes=VMEM_LIMIT),
    )(cvec, w_ada, b_ada.reshape(depth, 1, 3 * D_MODEL))


def _rope(z, cos, sin):
    lane = lax.broadcasted_iota(jnp.int32, z.shape, 1)
    first = (lane % (2 * 16)) < 16
    partner = jnp.where(first, pltpu.roll(z, LANES - 16, 1), pltpu.roll(z, 16, 1))
    return z * cos + partner * sin


def _attend(get_block, n_blocks, sink2_of_head, store, fillers, after_block):
    low = lax.broadcasted_iota(jnp.int32, (QBLK, LANES), 1) < HEAD_DIM
    n_kv = N_HEADS // GQA
    tasks = [(b, g) for b in range(n_blocks) for g in range(n_kv)]

    def score(i):
        b, g = tasks[i]
        qz, k, ksw, _, _, biases = get_block(b)
        lhs = jnp.concatenate([qz[:, h * LANES:(h + 1) * LANES] for h in range(g * GQA, (g + 1) * GQA)], axis=0)
        s = lax.dot_general(lhs, ksw if g else k, (((1,), (1,)), ((), ())), preferred_element_type=F32)
        if not biases:
            return s
        blocks = [s[:, c * LANES:(c + 1) * LANES] for c in range(s.shape[1] // LANES)]
        return jnp.concatenate(
            [blk + jnp.concatenate([biases[c]] * GQA, axis=0) if c in biases else blk
             for c, blk in enumerate(blocks)], axis=1)

    pending = {i: score(i) for i in range(min(PIPE_DEPTH, len(tasks)))}
    for i, (b, g) in enumerate(tasks):
        if i + PIPE_DEPTH < len(tasks):
            pending[i + PIPE_DEPTH] = score(i + PIPE_DEPTH)
        s = pending.pop(i)
        es, sink_terms = [], []
        for j in range(GQA):
            for filler in fillers.get(GQA * i + j, ()):
                filler()
            sj = s[j * QBLK:(j + 1) * QBLK]
            sk2 = sink2_of_head(g * GQA + j)
            m = jnp.maximum(jnp.max(sj, axis=-1, keepdims=True), sk2)
            es.append(jnp.exp2(sj - m).astype(BF16))
            sink_terms.append(jnp.exp2(sk2 - m))
        _, _, _, v, vsw, _ = get_block(b)
        v_low, v_high = (vsw, v) if g else (v, vsw)
        low_k = lax.broadcasted_iota(jnp.int32, v.shape, 1) < HEAD_DIM
        ones = jnp.ones_like(v)
        o_even = jnp.dot(jnp.concatenate(es[0::2], axis=0), jnp.where(low_k, v_low, ones),
                         preferred_element_type=F32)
        o_odd = jnp.dot(jnp.concatenate(es[1::2], axis=0), jnp.where(low_k, ones, v_high),
                        preferred_element_type=F32)
        for p in range(GQA // 2):
            rows = slice(p * QBLK, (p + 1) * QBLK)
            num = jnp.where(low, o_even[rows], o_odd[rows])
            sums = pltpu.roll(jnp.where(low, o_odd[rows], o_even[rows]), HEAD_DIM, 1)
            den = sums + jnp.where(low, sink_terms[2 * p], sink_terms[2 * p + 1])
            store(b, g * (GQA // 2) + p, num * (1.0 / den))
        if g == n_kv - 1:
            after_block(b)


def _pool_group(g, ug, u_prev, u_next, pos, seq_len, wp_ref):
    w = POOL_WINDOWS[g]
    n = ug.shape[0]
    pad = jnp.zeros((POOL_HALO, POOL_GROUP_W), F32)
    rows = n + 4 * POOL_HALO
    ext = jnp.concatenate([pad, u_prev, ug, u_next, pad], axis=0)
    s = ext + pltpu.roll(ext, 1, 0)
    step = 1
    while 2 * step < w:
        s = pltpu.roll(s, step, 0) + pltpu.roll(s, rows - step, 0)
        step *= 2
    ssum = s[2 * POOL_HALO:2 * POOL_HALO + n]
    lo = jnp.clip(pos - w // 2, 0, seq_len)
    hi = jnp.clip(pos + w // 2, 0, seq_len)
    pooled = ssum / (hi - lo).astype(F32) - ug
    return jnp.dot(pooled.astype(BF16), wp_ref[g], preferred_element_type=F32)


def _layer_kernel(*refs, rope, band, emit_kv, layer, final, seq_len, tiles_per_seq, n_tiles):
    it = iter(refs)
    take = lambda n: [next(it) for _ in range(n)]
    sink_ref, xn_ref, xr_ref, modn_ref, modr_ref, nw_ref, win_ref = take(7)
    cos_ref, sin_ref = take(2) if rope else (None, None)
    ck_ref, cv_ref = take(2) if band else (None, None)
    an_ref, pn_ref, ps_ref, wp_ref, wo_ref, fn_ref = take(6)
    (o_ref,) = take(1)
    k32_ref, v32_ref = take(2) if emit_kv else (None, None)
    hb_ref, qz_ref, kv_ref, ga_ref, u_ref, gp_ref = take(6)
    kvp_ref, up_ref = take(2) if band else (None, None)
    attn_ref, pool_ref, pcat_ref, acat_ref = take(4)
    state = {}

    tm = xn_ref.shape[0]
    n_blocks = tm // QBLK
    j = pl.program_id(0)
    new = j % 2
    cur = 1 - new
    t0 = (jnp.maximum(j - 1, 0) % tiles_per_seq) * tm

    qz_cur, kv_cur, ga_cur, u_cur, gp_cur = (r.at[cur] for r in (qz_ref, kv_ref, ga_ref, u_ref, gp_ref))
    qz_new, kv_new, ga_new, u_new, gp_new = (r.at[new] for r in (qz_ref, kv_ref, ga_ref, u_ref, gp_ref))


    def normalise():
        h = _rms(xn_ref[...], nw_ref[...]) * (1.0 + modn_ref[:, D_MODEL:2 * D_MODEL]) + modn_ref[:, 0:D_MODEL]
        hb_ref[...] = h.astype(BF16)

    def project(lo):
        return jnp.dot(hb_ref[...], win_ref[:, lo:lo + MXU_TILE], preferred_element_type=F32)

    def project_kv():
        kv = project(O_K)
        k, v = kv[:, :KV_WIDTH], kv[:, KV_WIDTH:]
        if emit_kv:
            k32_ref[...] = k
            v32_ref[...] = v
        if rope:
            k = _rope(k, cos_ref[...], sin_ref[...])
        kv_new[...] = jnp.concatenate(
            [k, pltpu.roll(k, HEAD_DIM, 1), v, pltpu.roll(v, HEAD_DIM, 1)], axis=1).astype(BF16)

    def project_q(c):
        def run():
            q = project(O_Q + c * MXU_TILE)
            low = lax.broadcasted_iota(jnp.int32, (tm, LANES), 1) < HEAD_DIM
            slots = []
            for p in range(MXU_TILE // LANES):
                both = q[:, p * LANES:(p + 1) * LANES]
                if rope:
                    both = _rope(both, cos_ref[...], sin_ref[...])
                both = both * (HEAD_DIM ** -0.5 * LOG2E)
                slots += [jnp.where(low, both, 0.0), jnp.where(low, pltpu.roll(both, HEAD_DIM, 1), 0.0)]
            width = 2 * MXU_TILE
            qz_new[:, c * width:(c + 1) * width] = jnp.concatenate(slots, axis=1).astype(BF16)
        return run

    def project_plain(dst, lo, c):
        def run():
            dst[:, c * MXU_TILE:(c + 1) * MXU_TILE] = project(lo + c * MXU_TILE)
        return run


    gate = modr_ref[:, 2 * D_MODEL:]
    rows_of = lambda c: slice(c * QBLK, (c + 1) * QBLK)
    kv_parts = lambda rows: [rows[:, i * KV_WIDTH:(i + 1) * KV_WIDTH] for i in range(4)]

    def get_block(c):
        qz = qz_cur[rows_of(c), :]
        if not band:
            per_seq = seq_len // QBLK
            first = (c // per_seq) * seq_len
            return (qz, *kv_parts(kv_cur[first:first + seq_len, :]), None)
        if c == 0:
            rows = jnp.concatenate([kvp_ref[...], kv_cur[0:2 * QBLK, :]], axis=0)
        elif c == n_blocks - 1:
            rows = jnp.concatenate([kv_cur[tm - 2 * QBLK:, :], kv_new[0:QBLK, :]], axis=0)
        else:
            rows = kv_cur[(c - 1) * QBLK:(c + 2) * QBLK, :]
        qi = lax.broadcasted_iota(jnp.int32, (QBLK, LANES), 0)
        kj = lax.broadcasted_iota(jnp.int32, (QBLK, LANES), 1)
        start = t0 + c * QBLK
        biases = {
            0: jnp.where((kj >= qi) & (start - WINDOW >= 0), 0.0, NEG_INF),
            2: jnp.where((kj <= qi) & (start + QBLK + WINDOW <= seq_len), 0.0, NEG_INF),
        }
        return (qz, *(jnp.concatenate([a, b], axis=0) for a, b in zip(kv_parts(rows), state["cache"])), biases)

    def pool_group(g, first, n):
        def run():
            sl = slice(g * POOL_GROUP_W, (g + 1) * POOL_GROUP_W)
            if band:
                pos = t0 + first + lax.broadcasted_iota(jnp.int32, (n, 1), 0)
                u_prev = jnp.where(t0 > 0, up_ref[:, sl], 0.0)
                u_next = jnp.where(t0 + tm < seq_len, u_new[0:POOL_HALO, sl], 0.0)
            else:
                pos = lax.broadcasted_iota(jnp.int32, (n, 1), 0)
                u_prev = u_next = jnp.zeros((POOL_HALO, POOL_GROUP_W), F32)
            y = _pool_group(g, u_cur[first:first + n, sl], u_prev, u_next, pos, seq_len, wp_ref)
            pool_ref[first:first + n, sl] = y * ps_ref[:, sl]
        return run

    def pool_gate(c):
        def run():
            r = rows_of(c)
            pcat_ref[r, :] = (_rms(pool_ref[r, :], pn_ref[...]) * _silu(gp_cur[r, :])).astype(BF16)
        return run

    def pool_projection(c):
        def run():
            cols = slice(c * MXU_TILE, (c + 1) * MXU_TILE)
            y = jnp.dot(pcat_ref[...], wo_ref[ATTN_WIDTH:, cols], preferred_element_type=F32)
            o_ref[:, cols] = xr_ref[:, cols] + gate[:, cols] * y
        return run

    def attn_gate(c):
        r = rows_of(c)
        acat_ref[r, :] = (_rms(attn_ref[r, :], an_ref[...]) * _silu(ga_cur[r, :])).astype(BF16)

    def store(c, pair, tile):
        attn_ref[rows_of(c), pair * LANES:(pair + 1) * LANES] = tile

    def run_step(do_project, do_mix):
        halves = range(ATTN_WIDTH // MXU_TILE)
        if do_project:
            proj_first = [normalise, project_kv] + [project_plain(u_new, O_U, c) for c in halves]
            proj_gp = [project_plain(gp_new, O_GP, c) for c in halves]
            proj_rest = [project_q(c) for c in halves] + [project_plain(ga_new, O_GA, c) for c in halves]
        else:
            proj_first = proj_gp = proj_rest = []
        if not do_mix:
            for f in proj_first + proj_gp + proj_rest:
                f()
            return
        if band:
            kvp_ref[...] = kv_new[tm - QBLK:, :]
            up_ref[...] = u_new[tm - POOL_HALO:, :]
            ck = ck_ref[...]
            cv = cv_ref[...]
            state["cache"] = (ck.astype(BF16), pltpu.roll(ck, HEAD_DIM, 1).astype(BF16),
                              cv.astype(BF16), pltpu.roll(cv, HEAD_DIM, 1).astype(BF16))
        pool_rows = [(0, tm)] if band else [(s * seq_len, seq_len) for s in range(tm // seq_len)]
        work = (proj_first
                + [pool_group(g, first, n) for first, n in pool_rows for g in range(len(POOL_WINDOWS))]
                + proj_gp + [pool_gate(c) for c in range(n_blocks)] + proj_rest
                + [pool_projection(c) for c in range(D_MODEL // MXU_TILE)])
        n_slots = n_blocks * N_HEADS
        fillers = {}
        for i, f in enumerate(work):
            fillers.setdefault(i * (n_slots - 1) // len(work), []).append(f)
        _attend(get_block, n_blocks, lambda h: sink_ref[layer, h] * LOG2E, store, fillers, attn_gate)

        for c in range(D_MODEL // MXU_TILE):
            cols = slice(c * MXU_TILE, (c + 1) * MXU_TILE)
            y = jnp.dot(acat_ref[...], wo_ref[:ATTN_WIDTH, cols], preferred_element_type=F32)
            o_ref[:, cols] = o_ref[:, cols] + gate[:, cols] * y
        if final:
            o_ref[...] = _rms(o_ref[...], fn_ref[...])

    @pl.when(j == 0)
    def _():
        if band:
            kv_ref[1, tm - QBLK:, :] = jnp.zeros((QBLK, KV4_WIDTH), BF16)
            u_ref[1, tm - POOL_HALO:, :] = jnp.zeros((POOL_HALO, POOL_WIDTH), F32)
        run_step(True, False)

    @pl.when((j > 0) & (j < n_tiles))
    def _():
        run_step(True, True)

    @pl.when(j == n_tiles)
    def _():
        run_step(False, True)


def _layer(x, mods, layer, row_of_tile, norm_w, w_in_b, rope_tabs, cache, params, sink, seq_len, final,
           emit_kv):
    n_tiles, tm, _ = x.shape
    rope = rope_tabs is not None
    band = cache is not None
    tiles_per_seq = max(seq_len // tm, 1)
    seq_in_tile = seq_len if seq_len < tm else tm
    assert tm % seq_in_tile == 0 and (seq_len <= tm or seq_len % tm == 0)
    proj_tile = lambda j: jnp.minimum(j, n_tiles - 1)
    mix_tile = lambda j: jnp.maximum(j - 1, 0)
    whole = lambda shape: pl.BlockSpec(shape, lambda j: (layer,) + (0,) * (len(shape) - 1))
    mod_spec = lambda tile_of: pl.BlockSpec(
        (None, None, 1, 3 * D_MODEL), lambda j: (layer, row_of_tile(tile_of(j)), 0, 0))
    in_specs = [
        pl.BlockSpec(memory_space=pltpu.SMEM),
        pl.BlockSpec((None, tm, D_MODEL), lambda j: (proj_tile(j), 0, 0)),
        pl.BlockSpec((None, tm, D_MODEL), lambda j: (mix_tile(j), 0, 0)),
        mod_spec(proj_tile), mod_spec(mix_tile),
        whole((None, 1, D_MODEL)),
        whole((None, D_MODEL, IN_WIDTH)),
    ]
    args = [sink, x, x, mods, mods, norm_w, w_in_b]
    if rope:
        in_specs += [pl.BlockSpec((tm, LANES), lambda j: (proj_tile(j) % tiles_per_seq, 0))] * 2
        args += list(rope_tabs)
    if band:
        n_ctx = cache[0].shape[2]
        in_specs += [pl.BlockSpec((None, None, n_ctx, KV_WIDTH),
                                  lambda j: (mix_tile(j) // tiles_per_seq, layer, 0, 0))] * 2
        args += list(cache)
    in_specs += [
        whole((None, 1, ATTN_WIDTH)), whole((None, 1, POOL_WIDTH)), whole((None, 1, POOL_WIDTH)),
        whole((None, len(POOL_WINDOWS), POOL_GROUP_W, POOL_GROUP_W)),
        whole((None, D_MODEL, D_MODEL)),
        pl.BlockSpec((1, D_MODEL), lambda j: (0, 0)),
    ]
    args += list(params)
    out_specs = [pl.BlockSpec((None, tm, D_MODEL), lambda j: (mix_tile(j), 0, 0))]
    out_shape = [jax.ShapeDtypeStruct(x.shape, F32)]
    if emit_kv:
        out_specs += [pl.BlockSpec((None, tm, KV_WIDTH), lambda j: (proj_tile(j), 0, 0))] * 2
        out_shape += [jax.ShapeDtypeStruct((n_tiles, tm, KV_WIDTH), F32)] * 2
    scratch = [pltpu.VMEM((tm, D_MODEL), BF16),
               pltpu.VMEM((2, tm, QZ_WIDTH), BF16), pltpu.VMEM((2, tm, KV4_WIDTH), BF16),
               pltpu.VMEM((2, tm, ATTN_WIDTH), F32), pltpu.VMEM((2, tm, POOL_WIDTH), F32),
               pltpu.VMEM((2, tm, POOL_WIDTH), F32)]
    if band:
        scratch += [pltpu.VMEM((QBLK, KV4_WIDTH), BF16), pltpu.VMEM((POOL_HALO, POOL_WIDTH), F32)]
    scratch += [pltpu.VMEM((tm, ATTN_WIDTH), F32), pltpu.VMEM((tm, POOL_WIDTH), F32),
                pltpu.VMEM((tm, POOL_WIDTH), BF16), pltpu.VMEM((tm, ATTN_WIDTH), BF16)]
    return pl.pallas_call(
        functools.partial(_layer_kernel, rope=rope, band=band, emit_kv=emit_kv, layer=layer, final=final,
                          seq_len=seq_len, tiles_per_seq=tiles_per_seq, n_tiles=n_tiles),
        grid=(n_tiles + 1,), in_specs=in_specs, out_specs=out_specs, out_shape=out_shape,
        scratch_shapes=scratch,
        compiler_params=pltpu.CompilerParams(dimension_semantics=("arbitrary",), vmem_limit_bytes=VMEM_LIMIT),
    )(*args)


def _rope_tables(seq):
    t = jnp.arange(seq)
    row = (t // GRID_W).astype(F32)
    col = (t % GRID_W).astype(F32)
    inv = ROPE_THETA ** (-(jnp.arange(0, AXIS_DIM, 2, dtype=F32) / AXIS_DIM))
    ang_r = row[:, None] * inv[None, :]
    ang_c = col[:, None] * inv[None, :]
    cos_h = jnp.concatenate([jnp.cos(ang_r)] * 2 + [jnp.cos(ang_c)] * 2, axis=1)
    sin_h = jnp.concatenate([-jnp.sin(ang_r), jnp.sin(ang_r), -jnp.sin(ang_c), jnp.sin(ang_c)], axis=1)
    return jnp.tile(cos_h, (1, LANES // HEAD_DIM)), jnp.tile(sin_h, (1, LANES // HEAD_DIM))


def kernel(x_prompt, x_sample, cache_k, cache_v, c, c_ctx, norm_w, w_ada, b_ada, w_in, sink, attn_norm,
           pool_norm, w_pool, pool_scale, w_out, final_norm):
    depth = w_in.shape[0]
    n_ctx_b, ctx_len, _ = x_prompt.shape
    n_lat_b, lat_len, _ = x_sample.shape
    assert n_lat_b + 1 <= 8

    cvec = jnp.concatenate([c_ctx[None, :], c, jnp.zeros((8 - 1 - n_lat_b, D_MODEL), F32)], axis=0)
    mods = _ada(cvec, w_ada, b_ada).reshape(depth, 8, 1, 3 * D_MODEL)

    w_in_b = w_in.astype(BF16)
    norm_w3 = norm_w.reshape(depth, 1, D_MODEL)
    params = (attn_norm.reshape(depth, 1, ATTN_WIDTH), pool_norm.reshape(depth, 1, POOL_WIDTH),
              pool_scale.reshape(depth, 1, POOL_WIDTH), w_pool.astype(BF16), w_out.astype(BF16),
              final_norm.reshape(1, D_MODEL))
    cache = (cache_k.reshape(n_lat_b, depth, cache_k.shape[2], KV_WIDTH),
             cache_v.reshape(n_lat_b, depth, cache_v.shape[2], KV_WIDTH))
    rope_tabs = _rope_tables(lat_len)
    lat_tiles_per_seq = lat_len // TILE

    xp = x_prompt.reshape(n_ctx_b * ctx_len // TILE, TILE, D_MODEL)
    xs = x_sample.reshape(n_lat_b * lat_len // TILE, TILE, D_MODEL)
    new_k, new_v = [], []
    for l in range(depth):
        final = l == depth - 1
        xp, k32, v32 = _layer(xp, mods, l, lambda t: 0, norm_w3, w_in_b, None, None, params, sink,
                              ctx_len, final, True)
        new_k.append(k32.reshape(n_ctx_b, ctx_len, KV_WIDTH))
        new_v.append(v32.reshape(n_ctx_b, ctx_len, KV_WIDTH))
        (xs,) = _layer(xs, mods, l, lambda t: 1 + t // lat_tiles_per_seq, norm_w3, w_in_b, rope_tabs, cache,
                       params, sink, lat_len, final, False)

    kv_shape = (n_ctx_b, depth, ctx_len, KV_WIDTH // HEAD_DIM, HEAD_DIM)
    k_state = jnp.stack(new_k, axis=1).reshape(kv_shape)
    v_state = jnp.stack(new_v, axis=1).reshape(kv_shape)
    return (xp.reshape(x_prompt.shape), xs.reshape(x_sample.shape), k_state, v_state)
```
